```python
import math
import jax
import jax.numpy as jnp
from jax import lax
import numpy as np

D_MODEL = 1024
BATCH = 2
SEQ = 16384
DEPTH = 2

N_META = 16
CHUNK = 64
EPS = 1e-6

HG_W = D_MODEL // 4
HG_DK = 64
HG_DV = 64
HG_HEADS = HG_W // HG_DK
GDN_W = D_MODEL // 2
GDN_DK = 128
GDN_DV = 128
GDN_HEADS = GDN_W // GDN_DK
GDN_CONV = 4
S5_CH = D_MODEL // 4
S5_GROUP = 16
S5_NGROUPS = S5_CH // S5_GROUP
S5_STATE = 64
D_MIX = HG_W + GDN_W + S5_CH
D_FF = 4 * D_MODEL

IN_SPLITS = (HG_W, HG_W, HG_W, HG_W,
             GDN_W, GDN_W, GDN_W, GDN_W,
             GDN_HEADS, GDN_HEADS,
             S5_CH)
N_IN = sum(IN_SPLITS)

kernel_name = "hymba_style_hgrn2_gdn_s5_hybrid"


def split_points():
    pts, acc = [], 0
    for s in IN_SPLITS[:-1]:
        acc += s
        pts.append(acc)
    return pts


def rmsnorm(x, w):
    xf = x.astype(jnp.float32)
    y = xf * lax.rsqrt(jnp.mean(xf * xf, axis=-1, keepdims=True) + EPS)
    return (y * w.astype(jnp.float32)).astype(x.dtype)


def l2norm(x):
    return x * lax.rsqrt(jnp.sum(x * x, axis=-1, keepdims=True) + EPS)


def to_heads(t, n_heads):
    b, l, _ = t.shape
    return t.reshape(b, l, n_heads, -1).transpose(0, 2, 1, 3)


def from_heads(t):
    b, h, l, d = t.shape
    return t.transpose(0, 2, 1, 3).reshape(b, l, h * d)


def to_chunks(t, chunk):
    b, h, T = t.shape[:3]
    return t.reshape(b, h, T // chunk, chunk, *t.shape[3:])


def causal_depthwise_conv(x, w):
    k, c = w.shape
    return lax.conv_general_dilated(
        x, w[:, None, :].astype(x.dtype), window_strides=(1,), padding=[(k - 1, 0)],
        dimension_numbers=("NWC", "WIO", "NWC"), feature_group_count=c)


def run_meta_then_chunks(segment_fn, arrays, state0):
    head = [t[:, :, :N_META] for t in arrays]
    tail = [t[:, :, N_META:] for t in arrays]
    o_meta, state = segment_fn(*head, state0, N_META)
    o_tail, _ = segment_fn(*tail, state, CHUNK)
    return jnp.concatenate([o_meta, o_tail], axis=2)


def hgrn2_segment(q, k, v, log_f, state0, chunk):
    b, h, T, _ = q.shape
    xs = tuple(jnp.moveaxis(to_chunks(t, chunk), 2, 0) for t in (q, k, v, log_f))
    causal = jnp.tril(jnp.ones((chunk, chunk), dtype=bool))[:, :, None]

    def step(S, inp):
        qi, ki, vi, gi = inp
        cum = jnp.cumsum(gi, axis=2)
        diff = cum[:, :, :, None, :] - cum[:, :, None, :, :]
        decay = jnp.where(causal, jnp.exp(jnp.where(causal, diff, 0.0)), 0.0)
        scores = jnp.einsum("bhtd,bhsd,bhtsd->bhts", qi, ki, decay)
        o = (jnp.einsum("bhts,bhse->bhte", scores, vi)
             + jnp.einsum("bhtd,bhde->bhte", qi * jnp.exp(cum), S))
        last = cum[:, :, -1:, :]
        S = (S * jnp.exp(last)[:, :, 0, :, None]
             + jnp.einsum("bhsd,bhse->bhde", ki * jnp.exp(last - cum), vi))
        return S, o

    S, o = lax.scan(step, state0, xs)
    o = jnp.moveaxis(o, 0, 2).reshape(b, h, T, -1)
    return o, S


def hgrn2_mixer(q, f_logit, i, gate, lb, norm_w):
    f32 = jnp.float32
    bsz = q.shape[0]
    fl = f_logit.astype(f32)
    lbf = lb.astype(f32)
    log_f = jnp.logaddexp(jax.nn.log_sigmoid(fl), jnp.log(lbf) + jax.nn.log_sigmoid(-fl))
    k = (1.0 - lbf) * jax.nn.sigmoid(-fl)
    qh = to_heads(jax.nn.silu(q.astype(f32)), HG_HEADS)
    kh = to_heads(k, HG_HEADS)
    vh = to_heads(i.astype(f32), HG_HEADS)
    gh = to_heads(log_f, HG_HEADS)
    s0 = jnp.zeros((bsz, HG_HEADS, HG_DK, HG_DV), f32)
    o = run_meta_then_chunks(hgrn2_segment, [qh, kh, vh, gh], s0)
    o = rmsnorm(o, norm_w) * jax.nn.silu(to_heads(gate.astype(f32), HG_HEADS))
    return from_heads(o)


def gdn_segment(q, k, v, g, beta, state0, chunk):
    b, h, T, _ = q.shape
    qc, kc, vc = (to_chunks(t, chunk) for t in (q, k, v))
    gc, bc = to_chunks(g, chunk), to_chunks(beta, chunk)
    cum = jnp.cumsum(gc, axis=-1)
    incl = jnp.tril(jnp.ones((chunk, chunk), dtype=bool))
    strict = jnp.tril(jnp.ones((chunk, chunk), dtype=bool), k=-1)
    diff = cum[..., :, None] - cum[..., None, :]
    decay = jnp.where(incl, jnp.exp(jnp.where(incl, diff, 0.0)), 0.0)
    kb = kc * bc[..., None]
    m = jnp.where(strict, jnp.einsum("bhntd,bhnsd->bhnts", kb, kc) * decay, 0.0)
    eye = jnp.eye(chunk, dtype=m.dtype)
    tmat = lax.linalg.triangular_solve(eye + m, jnp.broadcast_to(eye, m.shape),
                                       left_side=True, lower=True, unit_diagonal=True)
    u = jnp.einsum("bhnts,bhnse->bhnte", tmat, vc * bc[..., None])
    w = jnp.einsum("bhnts,bhnsd->bhntd", tmat, kb * jnp.exp(cum)[..., None])
    a_intra = jnp.einsum("bhntd,bhnsd->bhnts", qc, kc) * decay
    q_dec = qc * jnp.exp(cum)[..., None]
    k_dec = kc * jnp.exp(cum[..., -1:] - cum)[..., None]
    chunk_decay = jnp.exp(cum[..., -1])
    xs = tuple(jnp.moveaxis(t, 2, 0) for t in (u, w, a_intra, q_dec, k_dec, chunk_decay))

    def step(S, inp):
        ui, wi, ai, qi, ki, di = inp
        v_new = ui - jnp.einsum("bhtd,bhde->bhte", wi, S)
        o = jnp.einsum("bhtd,bhde->bhte", qi, S) + jnp.einsum("bhts,bhse->bhte", ai, v_new)
        S = S * di[..., None, None] + jnp.einsum("bhtd,bhte->bhde", ki, v_new)
        return S, o

    S, o = lax.scan(step, state0, xs)
    o = jnp.moveaxis(o, 0, 2).reshape(b, h, T, -1)
    return o, S


def gdn_mixer(q, k, v, z, a, b, conv_w, a_log, dt_bias, norm_w):
    f32 = jnp.float32
    bsz = q.shape[0]
    qkv = jax.nn.silu(causal_depthwise_conv(jnp.concatenate([q, k, v], axis=-1), conv_w)).astype(f32)
    qc, kc, vc = jnp.split(qkv, 3, axis=-1)
    qh = l2norm(to_heads(qc, GDN_HEADS)) * (GDN_DK ** -0.5)
    kh = l2norm(to_heads(kc, GDN_HEADS))
    vh = to_heads(vc, GDN_HEADS)
    g = -jnp.exp(a_log.astype(f32)) * jax.nn.softplus(a.astype(f32) + dt_bias.astype(f32))
    beta = jax.nn.sigmoid(b.astype(f32))
    g = g.transpose(0, 2, 1)
    beta = beta.transpose(0, 2, 1)
    s0 = jnp.zeros((bsz, GDN_HEADS, GDN_DK, GDN_DV), f32)
    o = run_meta_then_chunks(gdn_segment, [qh, kh, vh, g, beta], s0)
    o = rmsnorm(o, norm_w) * jax.nn.silu(to_heads(z.astype(f32), GDN_HEADS))
    return from_heads(o)


def s5_combine(e1, e2):
    a1r, a1i, b1r, b1i = e1
    a2r, a2i, b2r, b2i = e2
    ar = a2r * a1r - a2i * a1i
    ai = a2r * a1i + a2i * a1r
    br = a2r * b1r - a2i * b1i + b2r
    bi = a2r * b1i + a2i * b1r + b2i
    return ar, ai, br, bi


def s5_mixer(u, lam_re, lam_im, log_step, b_re, b_im, c_re, c_im, d, glu_w, norm_w):
    f32 = jnp.float32
    bsz, l, _ = u.shape
    u_flat = u.astype(f32)
    ug = u_flat.reshape(bsz, l, S5_NGROUPS, S5_GROUP)
    lre = jnp.minimum(lam_re.astype(f32), -1e-4)
    lim = lam_im.astype(f32)
    dt = jnp.exp(log_step.astype(f32))[:, None]
    mag = jnp.exp(lre * dt)
    abar_re = mag * jnp.cos(lim * dt)
    abar_im = mag * jnp.sin(lim * dt)
    den = lre * lre + lim * lim
    nr = abar_re - 1.0
    ni = abar_im
    coef_re = ((nr * lre + ni * lim) / den)[..., None]
    coef_im = ((ni * lre - nr * lim) / den)[..., None]
    bbar_re = coef_re * b_re.astype(f32) - coef_im * b_im.astype(f32)
    bbar_im = coef_re * b_im.astype(f32) + coef_im * b_re.astype(f32)
    bu_re = jnp.einsum("blgc,gpc->blgp", ug, bbar_re)
    bu_im = jnp.einsum("blgc,gpc->blgp", ug, bbar_im)
    a_re = jnp.broadcast_to(abar_re, bu_re.shape)
    a_im = jnp.broadcast_to(abar_im, bu_im.shape)
    _, _, x_re, x_im = lax.associative_scan(s5_combine, (a_re, a_im, bu_re, bu_im), axis=1)
    y = (jnp.einsum("gcp,blgp->blgc", c_re.astype(f32), x_re)
         - jnp.einsum("gcp,blgp->blgc", c_im.astype(f32), x_im))
    y = y.reshape(bsz, l, S5_CH) + d.astype(f32) * u_flat
    zg = jax.nn.gelu(y)
    out = zg * jax.nn.sigmoid(zg @ glu_w.astype(f32))
    return rmsnorm(out, norm_w)


def setup_inputs(seed: int = 0) -> dict:
    key = jax.random.key(seed)
    ks = jax.random.split(key, 32)
    f32 = jnp.float32

    def nrm(k, shape, scale):
        return scale * jax.random.normal(k, shape, f32)

    def gain(k, shape):
        return 1.0 + 0.02 * jax.random.normal(k, shape, f32)

    x = nrm(ks[0], (BATCH, SEQ, D_MODEL), 1.0)
    meta = nrm(ks[1], (N_META, D_MODEL), 1.0)
    norm_mix_w = gain(ks[2], (DEPTH, D_MODEL))
    norm_mlp_w = gain(ks[3], (DEPTH, D_MODEL))
    w_in = nrm(ks[4], (DEPTH, D_MODEL, N_IN), D_MODEL ** -0.5)
    hgrn_lb_logits = nrm(ks[5], (DEPTH, HG_W), 0.1)
    hgrn_norm_w = gain(ks[6], (DEPTH, HG_DV))
    gdn_conv_w = nrm(ks[7], (DEPTH, GDN_CONV, 3 * GDN_W), GDN_CONV ** -0.5)
    gdn_a_log = jnp.log(jax.random.uniform(ks[8], (DEPTH, GDN_HEADS), f32, 1.0, 16.0))
    dt0 = jnp.exp(jax.random.uniform(ks[9], (DEPTH, GDN_HEADS), f32, math.log(1e-3), math.log(1e-1)))
    gdn_dt_bias = dt0 + jnp.log(-jnp.expm1(-dt0))
    gdn_norm_w = gain(ks[10], (DEPTH, GDN_DV))
    s5_lam_re = -0.5 + 0.01 * jax.random.normal(ks[11], (DEPTH, S5_NGROUPS, S5_STATE), f32)
    s5_lam_im = (jnp.pi * jnp.arange(S5_STATE, dtype=f32)[None, None, :]
                 + 0.01 * jax.random.normal(ks[12], (DEPTH, S5_NGROUPS, S5_STATE), f32))
    s5_log_step = jax.random.uniform(ks[13], (DEPTH, S5_NGROUPS), f32, math.log(1e-3), math.log(1e-1))
    s5_b_re = nrm(ks[14], (DEPTH, S5_NGROUPS, S5_STATE, S5_GROUP), (2 * S5_GROUP) ** -0.5)
    s5_b_im = nrm(ks[15], (DEPTH, S5_NGROUPS, S5_STATE, S5_GROUP), (2 * S5_GROUP) ** -0.5)
    s5_c_re = nrm(ks[16], (DEPTH, S5_NGROUPS, S5_GROUP, S5_STATE), (2 * S5_STATE) ** -0.5)
    s5_c_im = nrm(ks[17], (DEPTH, S5_NGROUPS, S5_GROUP, S5_STATE), (2 * S5_STATE) ** -0.5)
    s5_d = nrm(ks[18], (DEPTH, S5_CH), 1.0)
    s5_glu_w = nrm(ks[19], (DEPTH, S5_CH, S5_CH), S5_CH ** -0.5)
    s5_norm_w = gain(ks[20], (DEPTH, S5_CH))
    w_out = nrm(ks[21], (DEPTH, D_MIX, D_MODEL), D_MIX ** -0.5)
    mlp_w1 = nrm(ks[22], (DEPTH, D_MODEL, D_FF), D_MODEL ** -0.5)
    mlp_w2 = nrm(ks[23], (DEPTH, D_FF, D_MODEL), D_FF ** -0.5)
    final_norm_w = gain(ks[24], (D_MODEL,))
    return {
        "x": x, "meta": meta, "norm_mix_w": norm_mix_w, "norm_mlp_w": norm_mlp_w,
        "w_in": w_in, "hgrn_lb_logits": hgrn_lb_logits, "hgrn_norm_w": hgrn_norm_w,
        "gdn_conv_w": gdn_conv_w, "gdn_a_log": gdn_a_log, "gdn_dt_bias": gdn_dt_bias,
        "gdn_norm_w": gdn_norm_w, "s5_lam_re": s5_lam_re, "s5_lam_im": s5_lam_im,
        "s5_log_step": s5_log_step, "s5_b_re": s5_b_re, "s5_b_im": s5_b_im,
        "s5_c_re": s5_c_re, "s5_c_im": s5_c_im, "s5_d": s5_d, "s5_glu_w": s5_glu_w,
        "s5_norm_w": s5_norm_w, "w_out": w_out, "mlp_w1": mlp_w1, "mlp_w2": mlp_w2,
        "final_norm_w": final_norm_w,
    }


def reference(x, meta, norm_mix_w, norm_mlp_w, w_in, hgrn_lb_logits, hgrn_norm_w,
              gdn_conv_w, gdn_a_log, gdn_dt_bias, gdn_norm_w, s5_lam_re, s5_lam_im,
              s5_log_step, s5_b_re, s5_b_im, s5_c_re, s5_c_im, s5_d, s5_glu_w,
              s5_norm_w, w_out, mlp_w1, mlp_w2, final_norm_w):
    bsz = x.shape[0]
    meta_b = jnp.broadcast_to(meta[None].astype(x.dtype), (bsz, N_META, D_MODEL))
    h = jnp.concatenate([meta_b, x], axis=1)
    lb_table = jnp.cumsum(jax.nn.softmax(hgrn_lb_logits.astype(jnp.float32), axis=0), axis=0)
    lb_table = lb_table - lb_table[0]
    pts = split_points()
    for layer in range(DEPTH):
        hn = rmsnorm(h, norm_mix_w[layer])
        proj = hn @ w_in[layer]
        hq, hf, hi, hg, gq, gk, gv, gz, ga, gb, su = jnp.split(proj, pts, axis=-1)
        y_a = hgrn2_mixer(hq, hf, hi, hg, lb_table[layer], hgrn_norm_w[layer])
        y_b = gdn_mixer(gq, gk, gv, gz, ga, gb, gdn_conv_w[layer], gdn_a_log[layer],
                        gdn_dt_bias[layer], gdn_norm_w[layer])
        y_c = s5_mixer(su, s5_lam_re[layer], s5_lam_im[layer], s5_log_step[layer],
                       s5_b_re[layer], s5_b_im[layer], s5_c_re[layer], s5_c_im[layer],
                       s5_d[layer], s5_glu_w[layer], s5_norm_w[layer])
        mixed = jnp.concatenate([y_a, y_b, y_c], axis=-1).astype(h.dtype)
        h = h + mixed @ w_out[layer]
        hn = rmsnorm(h, norm_mlp_w[layer])
        h = h + jnp.square(jax.nn.relu(hn @ mlp_w1[layer])) @ mlp_w2[layer]
    return rmsnorm(h, final_norm_w)[:, N_META:]
```

```python
import functools

import jax
import jax.numpy as jnp
from jax import lax
from jax.experimental import pallas as pl
from jax.experimental.pallas import tpu as pltpu

F32 = jnp.float32
BF16 = jnp.bfloat16
HIGHEST = lax.Precision.HIGHEST

EPS = 1e-6
N_META = 16
D_MODEL = 1024
D_FF = 4 * D_MODEL

HG_HEADS = 4
HG_D = 64
HG_W = HG_HEADS * HG_D
HG_CHUNK = 16

GDN_HEADS = 4
GDN_D = 128
GDN_W = GDN_HEADS * GDN_D
GDN_CHUNK = 64
GDN_CONV = 4

S5_CH = 256
S5_GROUPS = 16
S5_GROUP_CH = 16
S5_STATE = 64
S5_BLOCK = 8
S5_ROW = S5_BLOCK * S5_CH
S5_NSTATE = S5_GROUPS * S5_STATE

TIME_BLOCK = 512
LANES = 128
VMEM_LIMIT = 56 * 1024 * 1024


def _dot(a, b, precision=None):
    return jnp.dot(a, b, preferred_element_type=F32, precision=precision)


def _dot_nt(a, b, precision=None):
    return lax.dot_general(a, b, (((1,), (1,)), ((), ())),
                           preferred_element_type=F32, precision=precision)


def _dot_tn(a, b, precision=None):
    return lax.dot_general(a, b, (((0,), (0,)), ((), ())),
                           preferred_element_type=F32, precision=precision)


def _rmsnorm(x, w):
    return x * lax.rsqrt(jnp.mean(x * x, axis=-1, keepdims=True) + EPS) * w


def _silu(x):
    return x * jax.nn.sigmoid(x)


def _log1p_exp_neg_abs(x):
    return jnp.log1p(jnp.exp(-jnp.abs(x)))


def _block_masks(n, block):
    r = lax.broadcasted_iota(jnp.int32, (n, n), 0)
    c = lax.broadcasted_iota(jnp.int32, (n, n), 1)
    shift = block.bit_length() - 1
    same = (r >> shift) == (c >> shift)
    tril = jnp.where(same, jnp.where(c <= r, 1.0, 0.0), 0.0).astype(F32)
    ones = jnp.where(same, 1.0, 0.0).astype(F32)
    return tril, ones


def _inproj_kernel(h_ref, nw_ref, wh_ref, wg_ref, ws_ref, wab_ref,
                   ph_ref, pg_ref, ps_ref, pab_ref):
    hn = _rmsnorm(h_ref[...], nw_ref[...]).astype(BF16)
    ph_ref[...] = _dot(hn, wh_ref[...])
    pg_ref[...] = _dot(hn, wg_ref[...])
    ps_ref[...] = _dot(hn, ws_ref[...])
    pab_ref[...] = _dot(hn, wab_ref[...])


def _const_spec(shape):
    return pl.BlockSpec(shape, lambda *_: (0,) * len(shape), pipeline_mode=pl.Buffered(1))


def _inproj(h2d, nw, wh, wg, ws, wab):
    m = h2d.shape[0]
    tm = TIME_BLOCK
    row = lambda w: pl.BlockSpec((tm, w), lambda i: (i, 0))
    return pl.pallas_call(
        _inproj_kernel,
        grid=(m // tm,),
        in_specs=[row(D_MODEL), _const_spec((1, D_MODEL)),
                  _const_spec(wh.shape), _const_spec(wg.shape),
                  _const_spec(ws.shape), _const_spec(wab.shape)],
        out_specs=[row(4 * HG_W), row(4 * GDN_W), row(S5_CH), row(LANES)],
        out_shape=[jax.ShapeDtypeStruct((m, 4 * HG_W), F32),
                   jax.ShapeDtypeStruct((m, 4 * GDN_W), F32),
                   jax.ShapeDtypeStruct((m, S5_CH), F32),
                   jax.ShapeDtypeStruct((m, LANES), F32)],
        compiler_params=pltpu.CompilerParams(
            dimension_semantics=("arbitrary",), vmem_limit_bytes=VMEM_LIMIT),
        name="inproj",
    )(h2d, nw, wh, wg, ws, wab)


def _hgrn_kernel(ph_ref, lb_ref, nw_ref, y_ref,
                 st_ref, qs_s, k_s, v_s, cum_s, qn_s, kn_s, dec_s, o_s):
    tb = ph_ref.shape[1]

    @pl.when(pl.program_id(1) == 0)
    def _():
        st_ref[...] = jnp.zeros_like(st_ref)

    q = ph_ref[0, :, 0:HG_W]
    fl = ph_ref[0, :, HG_W:2 * HG_W]
    lb = lb_ref[...]
    ls_pos = jnp.minimum(fl, 0.0) - _log1p_exp_neg_abs(fl)
    ls_neg = ls_pos - fl
    b = jnp.log(lb) + ls_neg
    log_f = jnp.maximum(ls_pos, b) + _log1p_exp_neg_abs(ls_pos - b)
    qs_s[...] = _silu(q)
    k_s[...] = (1.0 - lb) * jnp.exp(ls_neg)
    v_s[...] = ph_ref[0, :, 2 * HG_W:3 * HG_W]

    tril, ones = _block_masks(LANES, HG_CHUNK)
    for g in range(tb // LANES):
        sl = slice(g * LANES, (g + 1) * LANES)
        lf = log_f[sl]
        cum = _dot(tril, lf, HIGHEST)
        tot = _dot(ones, lf, HIGHEST)
        cum_s[sl, :] = cum
        qn_s[sl, :] = qs_s[sl, :] * jnp.exp(cum)
        kn_s[sl, :] = k_s[sl, :] * jnp.exp(tot - cum)
        dec_s[sl, :] = jnp.exp(tot)

    row = lax.broadcasted_iota(jnp.int32, (HG_CHUNK, LANES), 0)
    er = lax.broadcasted_iota(jnp.int32, (LANES, LANES), 0)
    ec = lax.broadcasted_iota(jnp.int32, (LANES, LANES), 1)
    head_bd = jnp.where((er >> 6) == (ec >> 6), 1.0, 0.0).astype(F32)
    head_bd_bf = head_bd.astype(BF16)

    def chunk(c, carry):
        r0 = pl.multiple_of(c * HG_CHUNK, HG_CHUNK)
        rows = pl.ds(r0, HG_CHUNK)
        for p in range(HG_W // LANES):
            ls = slice(p * LANES, (p + 1) * LANES)
            qn = qn_s[rows, ls]
            kn = kn_s[rows, ls]
            v = v_s[rows, ls]
            qs = qs_s[rows, ls]
            k = k_s[rows, ls]
            cum = cum_s[rows, ls]
            dec = dec_s[pl.ds(r0, 1), ls]
            st = st_ref[p]
            o = _dot_nt(qn.astype(BF16), st.astype(BF16))
            upd = _dot_tn(v.astype(BF16), kn.astype(BF16))
            st_ref[p] = st * dec + upd * head_bd
            ps = []
            for s in range(HG_CHUNK):
                e = jnp.exp(jnp.minimum(cum - cum[s:s + 1, :], 0.0))
                ps.append((qs * k[s:s + 1, :] * e).astype(BF16))
            sums = _dot(jnp.concatenate(ps, axis=0), head_bd_bf)
            for s in range(HG_CHUNK):
                sc = sums[s * HG_CHUNK:(s + 1) * HG_CHUNK]
                o = o + jnp.where(row >= s, sc * v[s:s + 1, :], 0.0)
            o_s[rows, ls] = o
        return carry

    lax.fori_loop(0, tb // HG_CHUNK, chunk, 0)

    o = o_s[...]
    gate = ph_ref[0, :, 3 * HG_W:4 * HG_W]
    r2 = lax.broadcasted_iota(jnp.int32, (HG_W, HG_W), 0)
    c2 = lax.broadcasted_iota(jnp.int32, (HG_W, HG_W), 1)
    mean_mat = jnp.where((r2 >> 6) == (c2 >> 6), 1.0 / HG_D, 0.0).astype(F32)
    ms = _dot(o * o, mean_mat, HIGHEST)
    y = o * lax.rsqrt(ms + EPS) * nw_ref[...] * _silu(gate)
    y_ref[0] = y.astype(y_ref.dtype)


def _hgrn(ph, lb, nw):
    b, lp, _ = ph.shape
    tb = TIME_BLOCK
    scr = lambda: pltpu.VMEM((tb, HG_W), F32)
    return pl.pallas_call(
        _hgrn_kernel,
        grid=(b, lp // tb),
        in_specs=[pl.BlockSpec((1, tb, 4 * HG_W), lambda i, j: (i, j, 0)),
                  _const_spec((1, HG_W)), _const_spec((1, HG_W))],
        out_specs=pl.BlockSpec((1, tb, HG_W), lambda i, j: (i, j, 0)),
        out_shape=jax.ShapeDtypeStruct((b, lp, HG_W), BF16),
        scratch_shapes=[pltpu.VMEM((HG_W // LANES, LANES, LANES), F32)] + [scr() for _ in range(8)],
        compiler_params=pltpu.CompilerParams(
            dimension_semantics=("arbitrary", "arbitrary"), vmem_limit_bytes=VMEM_LIMIT),
        name="hgrn2",
    )(ph, lb, nw)


def _unit_lower_inverse(a, eye):
    n = a.shape[0]
    x = eye - a
    p = a
    for _ in range(n.bit_length() - 2):
        p = _dot(p, p, HIGHEST)
        x = x + _dot(x, p, HIGHEST)
    return x


def _gdn_kernel(pg_ref, pab_ref, cw_ref, ab_ref, nw_ref, y_ref,
                s_ref, xpad, q_s, k_s, v_s, cumb_s, totb_s, betab_s, o_s):
    tb = pg_ref.shape[1]
    halo = 8

    @pl.when(pl.program_id(1) == 0)
    def _():
        s_ref[...] = jnp.zeros_like(s_ref)
        xpad[0:halo, :] = jnp.zeros((halo, 3 * GDN_W), F32)

    xpad[halo:halo + tb, :] = pg_ref[0, :, 0:3 * GDN_W]
    acc = cw_ref[GDN_CONV - 1:GDN_CONV, :] * xpad[halo:halo + tb, :]
    for j in range(GDN_CONV - 1):
        off = halo - (GDN_CONV - 1) + j
        acc = acc + cw_ref[j:j + 1, :] * xpad[off:off + tb, :]
    xpad[0:halo, :] = xpad[tb:tb + halo, :]
    c = _silu(acc)

    ones_d = jnp.ones((GDN_D, GDN_D), F32)
    for h in range(GDN_HEADS):
        ls = slice(h * GDN_D, (h + 1) * GDN_D)
        qh = c[:, h * GDN_D:(h + 1) * GDN_D]
        kh = c[:, GDN_W + h * GDN_D:GDN_W + (h + 1) * GDN_D]
        q_s[:, ls] = qh * lax.rsqrt(_dot(qh * qh, ones_d, HIGHEST) + EPS) * (GDN_D ** -0.5)
        k_s[:, ls] = kh * lax.rsqrt(_dot(kh * kh, ones_d, HIGHEST) + EPS)
    v_s[...] = c[:, 2 * GDN_W:3 * GDN_W]

    ab = pab_ref[0]
    sp = jnp.maximum(ab + ab_ref[1:2, :], 0.0) + _log1p_exp_neg_abs(ab + ab_ref[1:2, :])
    g_col = ab_ref[0:1, :] * sp
    beta = jax.nn.sigmoid(ab)
    tril, ones = _block_masks(LANES, GDN_CHUNK)
    for g in range(tb // LANES):
        sl = slice(g * LANES, (g + 1) * LANES)
        cum = _dot(tril, g_col[sl], HIGHEST)
        tot = _dot(ones, g_col[sl], HIGHEST)
        for h in range(GDN_HEADS):
            cumb_s[h, sl, :] = jnp.broadcast_to(cum[:, h:h + 1], (LANES, LANES))
            totb_s[h, sl, :] = jnp.broadcast_to(tot[:, h:h + 1], (LANES, LANES))
            betab_s[h, sl, :] = jnp.broadcast_to(
                beta[sl][:, GDN_HEADS + h:GDN_HEADS + h + 1], (LANES, LANES))

    n = GDN_CHUNK
    r = lax.broadcasted_iota(jnp.int32, (n, n), 0)
    cc = lax.broadcasted_iota(jnp.int32, (n, n), 1)
    incl = cc <= r
    strict = cc < r
    eye = jnp.where(cc == r, 1.0, 0.0).astype(F32)
    lane = lax.broadcasted_iota(jnp.int32, (n, LANES), 1)
    ones_nl = jnp.ones((n, LANES), F32)

    def chunk(ci, carry):
        r0 = pl.multiple_of(ci * n, n)
        rows = pl.ds(r0, n)
        for h in range(GDN_HEADS):
            ls = slice(h * GDN_D, (h + 1) * GDN_D)
            qh = q_s[rows, ls]
            kh = k_s[rows, ls]
            vh = v_s[rows, ls]
            cb = cumb_s[h, rows, :]
            tt = totb_s[h, rows, :]
            bb = betab_s[h, rows, :]
            eg = jnp.exp(cb)
            kb = kh * bb
            cum_t = cb[:, 0:n]
            cum_s = _dot_nt(ones_nl, jnp.where(lane == 0, cb, 0.0), HIGHEST)
            decay = jnp.exp(jnp.minimum(cum_t - cum_s, 0.0))
            kbf = kh.astype(BF16)
            a = jnp.where(strict, _dot_nt(kb.astype(BF16), kbf) * decay, 0.0)
            tmat = _unit_lower_inverse(a, eye).astype(BF16)
            u = _dot(tmat, (vh * bb).astype(BF16))
            w = _dot(tmat, (kb * eg).astype(BF16))
            a_intra = jnp.where(incl, _dot_nt(qh.astype(BF16), kbf) * decay, 0.0)
            s = s_ref[h]
            sbf = s.astype(BF16)
            v_new = u - _dot(w.astype(BF16), sbf)
            vnb = v_new.astype(BF16)
            o = _dot((qh * eg).astype(BF16), sbf) + _dot(a_intra.astype(BF16), vnb)
            k_dec = kh * jnp.exp(tt - cb)
            s_ref[h] = s * jnp.exp(tt[0:1, :]) + _dot_tn(k_dec.astype(BF16), vnb)
            o_s[rows, ls] = o
        return carry

    lax.fori_loop(0, tb // n, chunk, 0)

    mean_d = jnp.full((GDN_D, GDN_D), 1.0 / GDN_D, F32)
    for h in range(GDN_HEADS):
        ls = slice(h * GDN_D, (h + 1) * GDN_D)
        o = o_s[:, ls]
        z = pg_ref[0, :, 3 * GDN_W + h * GDN_D:3 * GDN_W + (h + 1) * GDN_D]
        ms = _dot(o * o, mean_d, HIGHEST)
        y_ref[0, :, ls] = (o * lax.rsqrt(ms + EPS) * nw_ref[:, ls] * _silu(z)).astype(y_ref.dtype)


def _gdn(pg, pab, cw, ab, nw):
    b, lp, _ = pg.shape
    tb = TIME_BLOCK
    wide = lambda: pltpu.VMEM((tb, GDN_W), F32)
    perhead = lambda: pltpu.VMEM((GDN_HEADS, tb, LANES), F32)
    return pl.pallas_call(
        _gdn_kernel,
        grid=(b, lp // tb),
        in_specs=[pl.BlockSpec((1, tb, 4 * GDN_W), lambda i, j: (i, j, 0)),
                  pl.BlockSpec((1, tb, LANES), lambda i, j: (i, j, 0)),
                  _const_spec(cw.shape), _const_spec(ab.shape), _const_spec(nw.shape)],
        out_specs=pl.BlockSpec((1, tb, GDN_W), lambda i, j: (i, j, 0)),
        out_shape=jax.ShapeDtypeStruct((b, lp, GDN_W), BF16),
        scratch_shapes=[pltpu.VMEM((GDN_HEADS, GDN_D, GDN_D), F32),
                        pltpu.VMEM((tb + 8, 3 * GDN_W), F32),
                        wide(), wide(), wide(), perhead(), perhead(), perhead(), wide()],
        compiler_params=pltpu.CompilerParams(
            dimension_semantics=("arbitrary", "arbitrary"), vmem_limit_bytes=VMEM_LIMIT),
        name="gdn",
    )(pg, pab, cw, ab, nw)


def _s5_tables(lam_re, lam_im, log_step, b_re, b_im, c_re, c_im, d):
    g, p, t = S5_GROUPS, S5_STATE, S5_BLOCK
    lre = jnp.minimum(lam_re.astype(F32), -1e-4)
    lim = lam_im.astype(F32)
    dt = jnp.exp(log_step.astype(F32))[:, None]
    mag = jnp.exp(lre * dt)
    ar = mag * jnp.cos(lim * dt)
    ai = mag * jnp.sin(lim * dt)
    den = lre * lre + lim * lim
    nr = ar - 1.0
    ni = ai
    coef_re = ((nr * lre + ni * lim) / den)[..., None]
    coef_im = ((ni * lre - nr * lim) / den)[..., None]
    bbr = coef_re * b_re.astype(F32) - coef_im * b_im.astype(F32)
    bbi = coef_re * b_im.astype(F32) + coef_im * b_re.astype(F32)
    cr = c_re.astype(F32)
    ci = c_im.astype(F32)
    prs, pis = [jnp.ones_like(ar)], [jnp.zeros_like(ai)]
    for _ in range(t):
        prs.append(prs[-1] * ar - pis[-1] * ai)
        pis.append(prs[-2] * ai + pis[-1] * ar)
    pr = jnp.stack(prs)
    pi = jnp.stack(pis)
    eye_g = jnp.eye(g, dtype=F32)
    wr = pr[:t, :, :, None] * bbr[None] - pi[:t, :, :, None] * bbi[None]
    wi = pr[:t, :, :, None] * bbi[None] + pi[:t, :, :, None] * bbr[None]
    kt = (jnp.einsum("gop,tgpc->tgoc", cr, wr, precision=HIGHEST)
          - jnp.einsum("gop,tgpc->tgoc", ci, wi, precision=HIGHEST))
    tt = jnp.arange(t)
    lag = tt[:, None] - tt[None, :]
    kts = jnp.where((lag >= 0)[:, :, None, None, None], kt[jnp.clip(lag, 0, t - 1)], 0.0)
    m = jnp.einsum("tsgoc,gh->sgctho", kts, eye_g).reshape(S5_ROW, S5_ROW)
    rev = jnp.arange(t - 1, -1, -1)
    win_r = jnp.einsum("sgpc,gh->sgchp", wr[rev], eye_g).reshape(S5_ROW, S5_NSTATE)
    win_i = jnp.einsum("sgpc,gh->sgchp", wi[rev], eye_g).reshape(S5_ROW, S5_NSTATE)
    win = jnp.concatenate([win_r, win_i], axis=1)
    er = cr[None] * pr[1:, :, None, :] - ci[None] * pi[1:, :, None, :]
    ei = cr[None] * pi[1:, :, None, :] + ci[None] * pr[1:, :, None, :]
    wout_r = jnp.einsum("tgop,gh->gptho", er, eye_g).reshape(S5_NSTATE, S5_ROW)
    wout_i = jnp.einsum("tgop,gh->gptho", -ei, eye_g).reshape(S5_NSTATE, S5_ROW)
    wout = jnp.concatenate([wout_r, wout_i], axis=0)
    a_r = pr[t].reshape(1, S5_NSTATE)
    a_i = pi[t].reshape(1, S5_NSTATE)
    step = jnp.concatenate([jnp.concatenate([a_r, a_r], axis=1),
                            jnp.concatenate([-a_i, a_i], axis=1)], axis=0)
    d_row = jnp.tile(d.astype(F32), t).reshape(1, S5_ROW)
    return m.astype(BF16), win.astype(BF16), wout.astype(BF16), step, d_row


def _s5_scan_kernel(u_ref, win_ref, step_ref, xs_ref, x_scr, bx_scr):
    rows = u_ref.shape[1]

    @pl.when(pl.program_id(1) == 0)
    def _():
        x_scr[...] = jnp.zeros_like(x_scr)

    bx_scr[...] = _dot(u_ref[0].astype(BF16), win_ref[...])
    a_same = step_ref[0:1, :]
    a_cross = step_ref[1:2, :]

    def body(r, x):
        xs_ref[0, pl.ds(r, 1), :] = x
        swapped = jnp.concatenate([x[:, S5_NSTATE:], x[:, :S5_NSTATE]], axis=1)
        return a_same * x + a_cross * swapped + bx_scr[pl.ds(r, 1), :]

    x_scr[...] = lax.fori_loop(0, rows, body, x_scr[...])


def _s5_out_kernel(u_ref, xs_ref, m_ref, wout_ref, d_ref, y_ref):
    tn = y_ref.shape[1]
    col = pl.multiple_of(pl.program_id(1) * tn, tn)
    y = _dot(u_ref[...].astype(BF16), m_ref[...])
    y = y + _dot(xs_ref[...].astype(BF16), wout_ref[...])
    y_ref[...] = y + d_ref[...] * u_ref[:, pl.ds(col, tn)]


def _s5(ps, tables):
    m_w, win, wout, step, d_row = tables
    b, lp, _ = ps.shape
    nb = lp // S5_BLOCK
    u = ps.reshape(b, nb, S5_ROW)
    rows = nb
    for cand in (1024, 768, 704, 512, 256, 128, 64, 32, 16, 8):
        if nb % cand == 0:
            rows = cand
            break
    xs = pl.pallas_call(
        _s5_scan_kernel,
        grid=(b, nb // rows),
        in_specs=[pl.BlockSpec((1, rows, S5_ROW), lambda i, j: (i, j, 0)),
                  _const_spec(win.shape), _const_spec(step.shape)],
        out_specs=pl.BlockSpec((1, rows, S5_ROW), lambda i, j: (i, j, 0)),
        out_shape=jax.ShapeDtypeStruct((b, nb, 2 * S5_NSTATE), F32),
        scratch_shapes=[pltpu.VMEM((1, 2 * S5_NSTATE), F32),
                        pltpu.VMEM((rows, 2 * S5_NSTATE), F32)],
        compiler_params=pltpu.CompilerParams(
            dimension_semantics=("arbitrary", "arbitrary"), vmem_limit_bytes=VMEM_LIMIT),
        name="s5_scan",
    )(u, win, step)
    u2 = u.reshape(b * nb, S5_ROW)
    xs2 = xs.reshape(b * nb, 2 * S5_NSTATE)
    tn = 512
    total = b * nb
    rt = rows if total % rows == 0 else total
    y = pl.pallas_call(
        _s5_out_kernel,
        grid=(total // rt, S5_ROW // tn),
        in_specs=[pl.BlockSpec((rt, S5_ROW), lambda i, j: (i, 0)),
                  pl.BlockSpec((rt, 2 * S5_NSTATE), lambda i, j: (i, 0)),
                  pl.BlockSpec((S5_ROW, tn), lambda i, j: (0, j)),
                  pl.BlockSpec((2 * S5_NSTATE, tn), lambda i, j: (0, j)),
                  pl.BlockSpec((1, tn), lambda i, j: (0, j))],
        out_specs=pl.BlockSpec((rt, tn), lambda i, j: (i, j)),
        out_shape=jax.ShapeDtypeStruct((total, S5_ROW), F32),
        compiler_params=pltpu.CompilerParams(
            dimension_semantics=("arbitrary", "arbitrary"), vmem_limit_bytes=VMEM_LIMIT),
        name="s5_out",
    )(u2, xs2, m_w, wout, d_row)
    return y.reshape(b * lp, S5_CH)


def _post_kernel(h_ref, ya_ref, yb_ref, yc_ref, glu_ref, s5nw_ref, wo_ref, nw_ref,
                 w1_ref, w2_ref, fnw_ref, o_ref, *, final):
    yc = yc_ref[...]
    zg = 0.5 * yc * (1.0 + jnp.tanh(0.7978845608028654 * (yc + 0.044715 * (yc * yc * yc))))
    out = zg * jax.nn.sigmoid(_dot(zg.astype(BF16), glu_ref[...]))
    y_c = _rmsnorm(out, s5nw_ref[...]).astype(BF16)
    h1 = h_ref[...]
    h1 = h1 + _dot(ya_ref[...], wo_ref[0:HG_W, :])
    h1 = h1 + _dot(yb_ref[...], wo_ref[HG_W:HG_W + GDN_W, :])
    h1 = h1 + _dot(y_c, wo_ref[HG_W + GDN_W:, :])
    hn = _rmsnorm(h1, nw_ref[...]).astype(BF16)
    mlp = None
    for j in range(D_FF // D_MODEL):
        sl = slice(j * D_MODEL, (j + 1) * D_MODEL)
        a = _dot(hn, w1_ref[:, sl])
        part = _dot(jnp.square(jnp.maximum(a, 0.0)).astype(BF16), w2_ref[sl, :])
        mlp = part if mlp is None else mlp + part
    acc = h1 + mlp
    if final:
        acc = _rmsnorm(acc, fnw_ref[...])
    o_ref[...] = acc


def _post(h2d, ya, yb, yc, glu, s5nw, wo, nw, w1, w2, fnw, final):
    m = h2d.shape[0]
    tm = TIME_BLOCK
    row = lambda w: pl.BlockSpec((tm, w), lambda i: (i, 0))
    return pl.pallas_call(
        functools.partial(_post_kernel, final=final),
        grid=(m // tm,),
        in_specs=[row(D_MODEL), row(HG_W), row(GDN_W), row(S5_CH),
                  _const_spec(glu.shape), _const_spec(s5nw.shape), _const_spec(wo.shape),
                  _const_spec(nw.shape), _const_spec(w1.shape), _const_spec(w2.shape),
                  _const_spec(fnw.shape)],
        out_specs=row(D_MODEL),
        out_shape=jax.ShapeDtypeStruct((m, D_MODEL), F32),
        compiler_params=pltpu.CompilerParams(
            dimension_semantics=("arbitrary",), vmem_limit_bytes=VMEM_LIMIT),
        name="post",
    )(h2d, ya, yb, yc, glu, s5nw, wo, nw, w1, w2, fnw)


def kernel(x, meta, norm_mix_w, norm_mlp_w, w_in, hgrn_lb_logits, hgrn_norm_w, gdn_conv_w, gdn_a_log, gdn_dt_bias, gdn_norm_w, s5_lam_re, s5_lam_im, s5_log_step, s5_b_re, s5_b_im, s5_c_re, s5_c_im, s5_d, s5_glu_w, s5_norm_w, w_out, mlp_w1, mlp_w2, final_norm_w):
    bsz, seq, d_model = x.shape
    assert d_model == D_MODEL
    depth = w_in.shape[0]
    l = N_META + seq
    lp = -(-l // TIME_BLOCK) * TIME_BLOCK
    meta_b = jnp.broadcast_to(meta[None].astype(x.dtype), (bsz, N_META, d_model))
    h = jnp.concatenate([meta_b, x, jnp.zeros((bsz, lp - l, d_model), x.dtype)], axis=1)
    h2d = h.reshape(bsz * lp, d_model)

    lb_table = jnp.cumsum(jax.nn.softmax(hgrn_lb_logits.astype(F32), axis=0), axis=0)
    lb_table = lb_table - lb_table[0]

    o_hg, o_gdn = 4 * HG_W, 4 * HG_W + 4 * GDN_W
    o_s5 = o_gdn + 2 * GDN_HEADS
    for layer in range(depth):
        w = w_in[layer]
        wh = w[:, :o_hg].astype(BF16)
        wg = w[:, o_hg:o_gdn].astype(BF16)
        wab = jnp.pad(w[:, o_gdn:o_s5], ((0, 0), (0, LANES - 2 * GDN_HEADS))).astype(BF16)
        ws = w[:, o_s5:].astype(BF16)
        ph, pg, ps, pab = _inproj(h2d, norm_mix_w[layer].reshape(1, -1), wh, wg, ws, wab)

        ya = _hgrn(ph.reshape(bsz, lp, -1), lb_table[layer].reshape(1, -1),
                   jnp.tile(hgrn_norm_w[layer].astype(F32), HG_HEADS).reshape(1, -1))

        ab = jnp.zeros((2, LANES), F32)
        ab = ab.at[0, :GDN_HEADS].set(-jnp.exp(gdn_a_log[layer].astype(F32)))
        ab = ab.at[1, :GDN_HEADS].set(gdn_dt_bias[layer].astype(F32))
        yb = _gdn(pg.reshape(bsz, lp, -1), pab.reshape(bsz, lp, -1),
                  gdn_conv_w[layer].astype(F32), ab,
                  jnp.tile(gdn_norm_w[layer].astype(F32), GDN_HEADS).reshape(1, -1))

        tables = _s5_tables(s5_lam_re[layer], s5_lam_im[layer], s5_log_step[layer],
                            s5_b_re[layer], s5_b_im[layer], s5_c_re[layer], s5_c_im[layer],
                            s5_d[layer])
        yc = _s5(ps.reshape(bsz, lp, -1), tables)

        h2d = _post(h2d, ya.reshape(bsz * lp, -1), yb.reshape(bsz * lp, -1), yc,
                    s5_glu_w[layer].astype(BF16), s5_norm_w[layer].reshape(1, -1).astype(F32),
                    w_out[layer].astype(BF16), norm_mlp_w[layer].reshape(1, -1).astype(F32),
                    mlp_w1[layer].astype(BF16), mlp_w2[layer].astype(BF16),
                    final_norm_w.reshape(1, -1).astype(F32), final=(layer == depth - 1))
    return h2d.reshape(bsz, lp, d_model)[:, N_META:l]
```

```python
import functools

import jax
import jax.numpy as jnp
from jax import lax
from jax.experimental import pallas as pl
from jax.experimental.pallas import tpu as pltpu

F32 = jnp.float32
BF16 = jnp.bfloat16
HIGHEST = lax.Precision.HIGHEST

EPS = 1e-6
N_META = 16
D_MODEL = 1024
D_FF = 4 * D_MODEL

HG_HEADS = 4
HG_D = 64
HG_W = HG_HEADS * HG_D
HG_CHUNK = 16
HG_UNROLL = 4

GDN_HEADS = 4
GDN_D = 128
GDN_W = GDN_HEADS * GDN_D
GDN_CHUNK = 64
GDN_CONV = 4
GDN_PRE_UNROLL = 4

S5_CH = 256
S5_GROUPS = 16
S5_GROUP_CH = 16
S5_STATE = 64
S5_BLOCK = 8
S5_ROW = S5_BLOCK * S5_CH
S5_NSTATE = S5_GROUPS * S5_STATE

TIME_BLOCK = 512
LANES = 128
VMEM_LIMIT = 56 * 1024 * 1024


def _dot(a, b, precision=None):
    return jnp.dot(a, b, preferred_element_type=F32, precision=precision)


def _dot_nt(a, b, precision=None):
    return lax.dot_general(a, b, (((1,), (1,)), ((), ())),
                           preferred_element_type=F32, precision=precision)


def _dot_tn(a, b, precision=None):
    return lax.dot_general(a, b, (((0,), (0,)), ((), ())),
                           preferred_element_type=F32, precision=precision)


def _rmsnorm(x, w):
    return x * lax.rsqrt(jnp.mean(x * x, axis=-1, keepdims=True) + EPS) * w


def _silu(x):
    return x * jax.nn.sigmoid(x)


def _log1p_exp_neg_abs(x):
    return jnp.log1p(jnp.exp(-jnp.abs(x)))


def _block_masks(n, block):
    r = lax.broadcasted_iota(jnp.int32, (n, n), 0)
    c = lax.broadcasted_iota(jnp.int32, (n, n), 1)
    shift = block.bit_length() - 1
    same = (r >> shift) == (c >> shift)
    tril = jnp.where(same, jnp.where(c <= r, 1.0, 0.0), 0.0).astype(F32)
    ones = jnp.where(same, 1.0, 0.0).astype(F32)
    return tril, ones


def _inproj_kernel(h_ref, nw_ref, wh_ref, wg_ref, ws_ref, wab_ref,
                   ph_ref, pg_ref, ps_ref, pab_ref):
    hn = _rmsnorm(h_ref[...], nw_ref[...]).astype(BF16)
    ph_ref[...] = _dot(hn, wh_ref[...])
    pg_ref[...] = _dot(hn, wg_ref[...])
    ps_ref[...] = _dot(hn, ws_ref[...])
    pab_ref[...] = _dot(hn, wab_ref[...])


def _const_spec(shape):
    return pl.BlockSpec(shape, lambda *_: (0,) * len(shape), pipeline_mode=pl.Buffered(1))


def _inproj(h2d, nw, wh, wg, ws, wab):
    m = h2d.shape[0]
    tm = TIME_BLOCK
    row = lambda w: pl.BlockSpec((tm, w), lambda i: (i, 0))
    return pl.pallas_call(
        _inproj_kernel,
        grid=(m // tm,),
        in_specs=[row(D_MODEL), _const_spec((1, D_MODEL)),
                  _const_spec(wh.shape), _const_spec(wg.shape),
                  _const_spec(ws.shape), _const_spec(wab.shape)],
        out_specs=[row(4 * HG_W), row(4 * GDN_W), row(S5_CH), row(LANES)],
        out_shape=[jax.ShapeDtypeStruct((m, 4 * HG_W), F32),
                   jax.ShapeDtypeStruct((m, 4 * GDN_W), F32),
                   jax.ShapeDtypeStruct((m, S5_CH), F32),
                   jax.ShapeDtypeStruct((m, LANES), F32)],
        compiler_params=pltpu.CompilerParams(
            dimension_semantics=("arbitrary",), vmem_limit_bytes=VMEM_LIMIT),
        name="inproj",
    )(h2d, nw, wh, wg, ws, wab)


def _hgrn_kernel(ph_ref, lb_ref, nw_ref, y_ref,
                 st_ref, qs_s, k_s, v_s, cum_s, qn_s, o_s):
    tb = ph_ref.shape[1]

    @pl.when(pl.program_id(1) == 0)
    def _():
        st_ref[...] = jnp.zeros_like(st_ref)

    q = ph_ref[0, :, 0:HG_W]
    fl = ph_ref[0, :, HG_W:2 * HG_W]
    lb = lb_ref[...]
    ls_pos = jnp.minimum(fl, 0.0) - _log1p_exp_neg_abs(fl)
    ls_neg = ls_pos - fl
    b = jnp.log(lb) + ls_neg
    log_f = jnp.maximum(ls_pos, b) + _log1p_exp_neg_abs(ls_pos - b)
    qs_s[...] = _silu(q)
    k_s[...] = (1.0 - lb) * jnp.exp(ls_neg)
    v_s[...] = ph_ref[0, :, 2 * HG_W:3 * HG_W]

    tril, _ = _block_masks(LANES, HG_CHUNK)
    for g in range(tb // LANES):
        sl = slice(g * LANES, (g + 1) * LANES)
        cum = _dot(tril, log_f[sl], HIGHEST)
        cum_s[sl, :] = cum
        qn_s[sl, :] = qs_s[sl, :] * jnp.exp(cum)

    row = lax.broadcasted_iota(jnp.int32, (HG_CHUNK, LANES), 0)
    er = lax.broadcasted_iota(jnp.int32, (LANES, LANES), 0)
    ec = lax.broadcasted_iota(jnp.int32, (LANES, LANES), 1)
    head_bd = jnp.where((er >> 6) == (ec >> 6), 1.0, 0.0).astype(F32)
    head_bd_bf = head_bd.astype(BF16)

    npair = HG_W // LANES

    def chunks(i, carry):
        items = [(c, p) for c in range(HG_UNROLL) for p in range(npair)]
        r0 = [pl.multiple_of((HG_UNROLL * i + c) * HG_CHUNK, HG_CHUNK) for c in range(HG_UNROLL)]
        rows = [pl.ds(r, HG_CHUNK) for r in r0]
        lanes = lambda p: slice(p * LANES, (p + 1) * LANES)
        m = range(len(items))
        qn = [qn_s[rows[c], lanes(p)] for c, p in items]
        v = [v_s[rows[c], lanes(p)] for c, p in items]
        qs = [qs_s[rows[c], lanes(p)] for c, p in items]
        k = [k_s[rows[c], lanes(p)] for c, p in items]
        cum = [cum_s[rows[c], lanes(p)] for c, p in items]
        last = [cum[j][HG_CHUNK - 1:HG_CHUNK, :] for j in m]
        dec = [jnp.exp(last[j]) for j in m]
        kn = [k[j] * jnp.exp(last[j] - cum[j]) for j in m]
        upd = [_dot_tn(v[j].astype(BF16), kn[j].astype(BF16)) * head_bd for j in m]
        sums = []
        for j in m:
            ps = []
            for s in range(HG_CHUNK):
                e = jnp.exp(jnp.minimum(cum[j] - cum[j][s:s + 1, :], 0.0))
                ps.append((qs[j] * k[j][s:s + 1, :] * e).astype(BF16))
            sums.append(_dot(jnp.concatenate(ps, axis=0), head_bd_bf))
        st = [st_ref[p] for p in range(npair)]
        o = []
        for j, (c, p) in enumerate(items):
            o.append(_dot_nt(qn[j].astype(BF16), st[p].astype(BF16)))
            st[p] = st[p] * dec[j] + upd[j]
        for p in range(npair):
            st_ref[p] = st[p]
        for j, (c, p) in enumerate(items):
            acc = o[j]
            for s in range(HG_CHUNK):
                sc = sums[j][s * HG_CHUNK:(s + 1) * HG_CHUNK]
                acc = acc + jnp.where(row >= s, sc * v[j][s:s + 1, :], 0.0)
            o_s[rows[c], lanes(p)] = acc
        return carry

    lax.fori_loop(0, tb // (HG_CHUNK * HG_UNROLL), chunks, 0)

    o = o_s[...]
    gate = ph_ref[0, :, 3 * HG_W:4 * HG_W]
    r2 = lax.broadcasted_iota(jnp.int32, (HG_W, HG_W), 0)
    c2 = lax.broadcasted_iota(jnp.int32, (HG_W, HG_W), 1)
    mean_mat = jnp.where((r2 >> 6) == (c2 >> 6), 1.0 / HG_D, 0.0).astype(BF16)
    ms = _dot((o * o).astype(BF16), mean_mat)
    y = o * lax.rsqrt(ms + EPS) * nw_ref[...] * _silu(gate)
    y_ref[0] = y.astype(y_ref.dtype)


def _hgrn(ph, lb, nw):
    b, lp, _ = ph.shape
    tb = TIME_BLOCK
    scr = lambda: pltpu.VMEM((tb, HG_W), F32)
    return pl.pallas_call(
        _hgrn_kernel,
        grid=(b, lp // tb),
        in_specs=[pl.BlockSpec((1, tb, 4 * HG_W), lambda i, j: (i, j, 0)),
                  _const_spec((1, HG_W)), _const_spec((1, HG_W))],
        out_specs=pl.BlockSpec((1, tb, HG_W), lambda i, j: (i, j, 0)),
        out_shape=jax.ShapeDtypeStruct((b, lp, HG_W), BF16),
        scratch_shapes=[pltpu.VMEM((HG_W // LANES, LANES, LANES), F32)] + [scr() for _ in range(6)],
        compiler_params=pltpu.CompilerParams(
            dimension_semantics=("arbitrary", "arbitrary"), vmem_limit_bytes=VMEM_LIMIT),
        name="hgrn2",
    )(ph, lb, nw)


def _gdn_kernel(pg_ref, pab_ref, cw_ref, ab_ref, nw_ref, y_ref,
                s_ref, xpad, q_s, k_s, v_s, cumb_s, betab_s, rowt_s,
                lhs_s, add_s, oo_s, o_s):
    tb = pg_ref.shape[1]
    n = GDN_CHUNK
    halo = 8

    @pl.when(pl.program_id(1) == 0)
    def _():
        s_ref[...] = jnp.zeros_like(s_ref)
        xpad[0:halo, :] = jnp.zeros((halo, 3 * GDN_W), F32)

    xpad[halo:halo + tb, :] = pg_ref[0, :, 0:3 * GDN_W]
    acc = cw_ref[GDN_CONV - 1:GDN_CONV, :] * xpad[halo:halo + tb, :]
    for j in range(GDN_CONV - 1):
        off = halo - (GDN_CONV - 1) + j
        acc = acc + cw_ref[j:j + 1, :] * xpad[off:off + tb, :]
    xpad[0:halo, :] = xpad[tb:tb + halo, :]
    c = _silu(acc)

    ones_d = jnp.ones((GDN_D, GDN_D), BF16)
    for h in range(GDN_HEADS):
        ls = slice(h * GDN_D, (h + 1) * GDN_D)
        qh = c[:, h * GDN_D:(h + 1) * GDN_D]
        kh = c[:, GDN_W + h * GDN_D:GDN_W + (h + 1) * GDN_D]
        q_s[:, ls] = qh * lax.rsqrt(_dot((qh * qh).astype(BF16), ones_d) + EPS) * (GDN_D ** -0.5)
        k_s[:, ls] = kh * lax.rsqrt(_dot((kh * kh).astype(BF16), ones_d) + EPS)
    v_s[...] = c[:, 2 * GDN_W:3 * GDN_W]

    ab = pab_ref[0]
    sp = jnp.maximum(ab + ab_ref[1:2, :], 0.0) + _log1p_exp_neg_abs(ab + ab_ref[1:2, :])
    g_col = ab_ref[0:1, :] * sp
    beta = jax.nn.sigmoid(ab)
    tril, _ = _block_masks(LANES, n)
    for g in range(tb // LANES):
        sl = slice(g * LANES, (g + 1) * LANES)
        cum = _dot(tril, g_col[sl], HIGHEST)
        for j in range(LANES // n):
            rowt_s[g * (LANES // n) + j] = cum[j * n:(j + 1) * n, :].T[0:8, :]
        for h in range(GDN_HEADS):
            cumb_s[h, sl, :] = jnp.broadcast_to(cum[:, h:h + 1], (LANES, LANES))
            betab_s[h, sl, :] = jnp.broadcast_to(
                beta[sl][:, GDN_HEADS + h:GDN_HEADS + h + 1], (LANES, LANES))

    r = lax.broadcasted_iota(jnp.int32, (n, n), 0)
    cc = lax.broadcasted_iota(jnp.int32, (n, n), 1)
    incl = cc <= r
    strict = cc < r
    eye = jnp.where(cc == r, 1.0, 0.0).astype(F32)

    def pre_body(i, carry):
        chunks = [GDN_PRE_UNROLL * i + c for c in range(GDN_PRE_UNROLL)]
        items = [(c, h) for c in range(GDN_PRE_UNROLL) for h in range(GDN_HEADS)]
        rows = [pl.ds(pl.multiple_of(ci * n, n), n) for ci in chunks]
        lanes = lambda h: slice(h * GDN_D, (h + 1) * GDN_D)
        qh = [q_s[rows[c], lanes(h)] for c, h in items]
        kh = [k_s[rows[c], lanes(h)] for c, h in items]
        vh = [v_s[rows[c], lanes(h)] for c, h in items]
        cb = [cumb_s[h, rows[c], :] for c, h in items]
        tt = [cb[j][n - 1:n, :] for j in range(len(items))]
        bb = [betab_s[h, rows[c], :] for c, h in items]
        cr = [rowt_s[chunks[c], h:h + 1, :] for c, h in items]
        m = range(len(items))
        eg = [jnp.exp(cb[j]) for j in m]
        kb = [kh[j] * bb[j] for j in m]
        decay = [jnp.exp(jnp.minimum(cb[j][:, 0:n] - cr[j], 0.0)) for j in m]
        both = [_dot_nt(jnp.concatenate([kb[j], qh[j]], axis=0).astype(BF16), kh[j].astype(BF16))
                for j in m]
        a_intra = [jnp.where(incl, both[j][n:2 * n] * decay[j], 0.0).astype(BF16) for j in m]
        p = [jnp.where(strict, both[j][0:n] * decay[j], 0.0) for j in m]
        x = [eye - p[j] for j in m]
        for _ in range(n.bit_length() - 2):
            pb = [p[j].astype(BF16) for j in m]
            p = [_dot(pb[j], pb[j]) for j in m]
            x = [x[j] + _dot(x[j].astype(BF16), p[j].astype(BF16)) for j in m]
        rhs = [jnp.concatenate([vh[j] * bb[j], kb[j] * eg[j]], axis=1).astype(BF16) for j in m]
        uw = [_dot(x[j].astype(BF16), rhs[j]).astype(BF16) for j in m]
        k_dec = [(kh[j] * jnp.exp(tt[j] - cb[j])).astype(BF16) for j in m]
        au_aw = [_dot(a_intra[j], uw[j]) for j in m]
        ku_kw = [_dot_tn(k_dec[j], uw[j]) for j in m]
        for j, (c, h) in enumerate(items):
            ci = chunks[c]
            oo_s[rows[c], lanes(h)] = au_aw[j][:, 0:GDN_D]
            add_s[ci, h] = ku_kw[j][:, 0:GDN_D]
            lhs_s[ci, h, 0:GDN_D, :] = (-ku_kw[j][:, GDN_D:]).astype(BF16)
            lhs_s[ci, h, GDN_D:GDN_D + n, :] = (qh[j] * eg[j] - au_aw[j][:, GDN_D:]).astype(BF16)
        return carry

    lax.fori_loop(0, tb // (GDN_PRE_UNROLL * n), pre_body, 0)

    def scan_body(ci, carry):
        r0 = pl.multiple_of(ci * n, n)
        rows = pl.ds(r0, n)
        hs = range(GDN_HEADS)
        s = [s_ref[h] for h in hs]
        d = [jnp.exp(cumb_s[h, pl.ds(r0 + n - 1, 1), :]) for h in hs]
        res = [_dot(lhs_s[ci, h], s[h].astype(BF16)) for h in hs]
        for h in hs:
            ls = slice(h * GDN_D, (h + 1) * GDN_D)
            o_s[rows, ls] = res[h][GDN_D:, :] + oo_s[rows, ls]
            s_ref[h] = s[h] * d[h] + res[h][0:GDN_D, :] + add_s[ci, h]
        return carry

    lax.fori_loop(0, tb // n, scan_body, 0)

    mean_d = jnp.full((GDN_D, GDN_D), 1.0 / GDN_D, BF16)
    for h in range(GDN_HEADS):
        ls = slice(h * GDN_D, (h + 1) * GDN_D)
        o = o_s[:, ls]
        z = pg_ref[0, :, 3 * GDN_W + h * GDN_D:3 * GDN_W + (h + 1) * GDN_D]
        ms = _dot((o * o).astype(BF16), mean_d)
        y_ref[0, :, ls] = (o * lax.rsqrt(ms + EPS) * nw_ref[:, ls] * _silu(z)).astype(y_ref.dtype)


def _gdn(pg, pab, cw, ab, nw):
    b, lp, _ = pg.shape
    tb = TIME_BLOCK
    nchunk = tb // GDN_CHUNK
    wide = lambda: pltpu.VMEM((tb, GDN_W), F32)
    perhead = lambda: pltpu.VMEM((GDN_HEADS, tb, LANES), F32)
    return pl.pallas_call(
        _gdn_kernel,
        grid=(b, lp // tb),
        in_specs=[pl.BlockSpec((1, tb, 4 * GDN_W), lambda i, j: (i, j, 0)),
                  pl.BlockSpec((1, tb, LANES), lambda i, j: (i, j, 0)),
                  _const_spec(cw.shape), _const_spec(ab.shape), _const_spec(nw.shape)],
        out_specs=pl.BlockSpec((1, tb, GDN_W), lambda i, j: (i, j, 0)),
        out_shape=jax.ShapeDtypeStruct((b, lp, GDN_W), BF16),
        scratch_shapes=[pltpu.VMEM((GDN_HEADS, GDN_D, GDN_D), F32),
                        pltpu.VMEM((tb + 8, 3 * GDN_W), F32),
                        wide(), wide(), wide(), perhead(), perhead(),
                        pltpu.VMEM((nchunk, 8, GDN_CHUNK), F32),
                        pltpu.VMEM((nchunk, GDN_HEADS, GDN_D + GDN_CHUNK, GDN_D), BF16),
                        pltpu.VMEM((nchunk, GDN_HEADS, GDN_D, GDN_D), F32),
                        wide(), wide()],
        compiler_params=pltpu.CompilerParams(
            dimension_semantics=("arbitrary", "arbitrary"), vmem_limit_bytes=VMEM_LIMIT),
        name="gdn",
    )(pg, pab, cw, ab, nw)


def _s5_tables(lam_re, lam_im, log_step, b_re, b_im, c_re, c_im, d):
    g, p, t = S5_GROUPS, S5_STATE, S5_BLOCK
    lre = jnp.minimum(lam_re.astype(F32), -1e-4)
    lim = lam_im.astype(F32)
    dt = jnp.exp(log_step.astype(F32))[:, None]
    mag = jnp.exp(lre * dt)
    ar = mag * jnp.cos(lim * dt)
    ai = mag * jnp.sin(lim * dt)
    den = lre * lre + lim * lim
    nr = ar - 1.0
    ni = ai
    coef_re = ((nr * lre + ni * lim) / den)[..., None]
    coef_im = ((ni * lre - nr * lim) / den)[..., None]
    bbr = coef_re * b_re.astype(F32) - coef_im * b_im.astype(F32)
    bbi = coef_re * b_im.astype(F32) + coef_im * b_re.astype(F32)
    cr = c_re.astype(F32)
    ci = c_im.astype(F32)
    prs, pis = [jnp.ones_like(ar)], [jnp.zeros_like(ai)]
    for _ in range(t):
        prs.append(prs[-1] * ar - pis[-1] * ai)
        pis.append(prs[-2] * ai + pis[-1] * ar)
    pr = jnp.stack(prs)
    pi = jnp.stack(pis)
    eye_g = jnp.eye(g, dtype=F32)
    wr = pr[:t, :, :, None] * bbr[None] - pi[:t, :, :, None] * bbi[None]
    wi = pr[:t, :, :, None] * bbi[None] + pi[:t, :, :, None] * bbr[None]
    kt = (jnp.einsum("gop,tgpc->tgoc", cr, wr, precision=HIGHEST)
          - jnp.einsum("gop,tgpc->tgoc", ci, wi, precision=HIGHEST))
    tt = jnp.arange(t)
    lag = tt[:, None] - tt[None, :]
    kts = jnp.where((lag >= 0)[:, :, None, None, None], kt[jnp.clip(lag, 0, t - 1)], 0.0)
    m = (kts.transpose(1, 2, 4, 0, 3)[:, :, :, :, None, :]
         * eye_g[None, :, None, None, :, None]).astype(BF16).reshape(S5_ROW, S5_ROW)
    rev = jnp.arange(t - 1, -1, -1)
    win = jnp.stack([wr[rev], wi[rev]], axis=2).transpose(0, 1, 4, 2, 3)
    win = (win[:, :, :, :, None, :] * eye_g[None, :, None, None, :, None]
           ).astype(BF16).reshape(S5_ROW, 2 * S5_NSTATE)
    er = cr[None] * pr[1:, :, None, :] - ci[None] * pi[1:, :, None, :]
    ei = cr[None] * pi[1:, :, None, :] + ci[None] * pr[1:, :, None, :]
    wout = jnp.stack([er, -ei], axis=0).transpose(0, 2, 4, 1, 3)
    wout = (wout[:, :, :, :, None, :] * eye_g[None, :, None, None, :, None]
            ).astype(BF16).reshape(2 * S5_NSTATE, S5_ROW)
    a_r = pr[t].reshape(1, S5_NSTATE)
    a_i = pi[t].reshape(1, S5_NSTATE)
    step = jnp.concatenate([jnp.concatenate([a_r, a_r], axis=1),
                            jnp.concatenate([-a_i, a_i], axis=1)], axis=0)
    d_row = jnp.tile(d.astype(F32), t).reshape(1, S5_ROW)
    return m, win, wout, step, d_row


def _s5_scan_kernel(u_ref, win_ref, step_ref, xs_ref, x_scr, bx_scr):
    rows = u_ref.shape[1]

    @pl.when(pl.program_id(1) == 0)
    def _():
        x_scr[...] = jnp.zeros_like(x_scr)

    bx_scr[...] = _dot(u_ref[0].astype(BF16), win_ref[...])
    a_same = step_ref[0:1, :]
    a_cross = step_ref[1:2, :]

    def body(r, x):
        xs_ref[0, pl.ds(r, 1), :] = x
        swapped = jnp.concatenate([x[:, S5_NSTATE:], x[:, :S5_NSTATE]], axis=1)
        return a_same * x + a_cross * swapped + bx_scr[pl.ds(r, 1), :]

    x_scr[...] = lax.fori_loop(0, rows, body, x_scr[...])


def _s5_out_kernel(u_ref, xs_ref, m_ref, wout_ref, d_ref, y_ref):
    tn = y_ref.shape[1]
    col = pl.multiple_of(pl.program_id(1) * tn, tn)
    y = _dot(u_ref[...].astype(BF16), m_ref[...])
    y = y + _dot(xs_ref[...].astype(BF16), wout_ref[...])
    y_ref[...] = y + d_ref[...] * u_ref[:, pl.ds(col, tn)]


def _s5(ps, tables):
    m_w, win, wout, step, d_row = tables
    b, lp, _ = ps.shape
    nb = lp // S5_BLOCK
    u = ps.reshape(b, nb, S5_ROW)
    rows = nb
    for cand in (1024, 768, 704, 512, 256, 128, 64, 32, 16, 8):
        if nb % cand == 0:
            rows = cand
            break
    xs = pl.pallas_call(
        _s5_scan_kernel,
        grid=(b, nb // rows),
        in_specs=[pl.BlockSpec((1, rows, S5_ROW), lambda i, j: (i, j, 0)),
                  _const_spec(win.shape), _const_spec(step.shape)],
        out_specs=pl.BlockSpec((1, rows, S5_ROW), lambda i, j: (i, j, 0)),
        out_shape=jax.ShapeDtypeStruct((b, nb, 2 * S5_NSTATE), F32),
        scratch_shapes=[pltpu.VMEM((1, 2 * S5_NSTATE), F32),
                        pltpu.VMEM((rows, 2 * S5_NSTATE), F32)],
        compiler_params=pltpu.CompilerParams(
            dimension_semantics=("arbitrary", "arbitrary"), vmem_limit_bytes=VMEM_LIMIT),
        name="s5_scan",
    )(u, win, step)
    u2 = u.reshape(b * nb, S5_ROW)
    xs2 = xs.reshape(b * nb, 2 * S5_NSTATE)
    tn = 512
    total = b * nb
    rt = rows if total % rows == 0 else total
    y = pl.pallas_call(
        _s5_out_kernel,
        grid=(total // rt, S5_ROW // tn),
        in_specs=[pl.BlockSpec((rt, S5_ROW), lambda i, j: (i, 0)),
                  pl.BlockSpec((rt, 2 * S5_NSTATE), lambda i, j: (i, 0)),
                  pl.BlockSpec((S5_ROW, tn), lambda i, j: (0, j)),
                  pl.BlockSpec((2 * S5_NSTATE, tn), lambda i, j: (0, j)),
                  pl.BlockSpec((1, tn), lambda i, j: (0, j))],
        out_specs=pl.BlockSpec((rt, tn), lambda i, j: (i, j)),
        out_shape=jax.ShapeDtypeStruct((total, S5_ROW), F32),
        compiler_params=pltpu.CompilerParams(
            dimension_semantics=("arbitrary", "arbitrary"), vmem_limit_bytes=VMEM_LIMIT),
        name="s5_out",
    )(u2, xs2, m_w, wout, d_row)
    return y.reshape(b * lp, S5_CH)


def _post_kernel(h_ref, ya_ref, yb_ref, yc_ref, glu_ref, s5nw_ref, wo_ref, nw_ref,
                 w1_ref, w2_ref, fnw_ref, o_ref, *, final):
    yc = yc_ref[...]
    zg = 0.5 * yc * (1.0 + jnp.tanh(0.7978845608028654 * (yc + 0.044715 * (yc * yc * yc))))
    out = zg * jax.nn.sigmoid(_dot(zg.astype(BF16), glu_ref[...]))
    y_c = _rmsnorm(out, s5nw_ref[...]).astype(BF16)
    h1 = h_ref[...]
    h1 = h1 + _dot(ya_ref[...], wo_ref[0:HG_W, :])
    h1 = h1 + _dot(yb_ref[...], wo_ref[HG_W:HG_W + GDN_W, :])
    h1 = h1 + _dot(y_c, wo_ref[HG_W + GDN_W:, :])
    hn = _rmsnorm(h1, nw_ref[...]).astype(BF16)
    mlp = None
    for j in range(D_FF // D_MODEL):
        sl = slice(j * D_MODEL, (j + 1) * D_MODEL)
        a = _dot(hn, w1_ref[:, sl])
        part = _dot(jnp.square(jnp.maximum(a, 0.0)).astype(BF16), w2_ref[sl, :])
        mlp = part if mlp is None else mlp + part
    acc = h1 + mlp
    if final:
        acc = _rmsnorm(acc, fnw_ref[...])
    o_ref[...] = acc


def _post(h2d, ya, yb, yc, glu, s5nw, wo, nw, w1, w2, fnw, final, drop_blocks=None):
    m = h2d.shape[0]
    tm = TIME_BLOCK
    row = lambda w: pl.BlockSpec((tm, w), lambda i: (i, 0))
    out_spec, out_rows = row(D_MODEL), m
    if drop_blocks is not None:
        per_seq, skip = drop_blocks
        out_rows = m // per_seq * (per_seq - skip)
        out_spec = pl.BlockSpec(
            (tm, D_MODEL),
            lambda i: ((i // per_seq) * (per_seq - skip) + jnp.maximum(i % per_seq - skip, 0), 0))
    return pl.pallas_call(
        functools.partial(_post_kernel, final=final),
        grid=(m // tm,),
        in_specs=[row(D_MODEL), row(HG_W), row(GDN_W), row(S5_CH),
                  _const_spec(glu.shape), _const_spec(s5nw.shape), _const_spec(wo.shape),
                  _const_spec(nw.shape), _const_spec(w1.shape), _const_spec(w2.shape),
                  _const_spec(fnw.shape)],
        out_specs=out_spec,
        out_shape=jax.ShapeDtypeStruct((out_rows, D_MODEL), F32),
        compiler_params=pltpu.CompilerParams(
            dimension_semantics=("arbitrary",), vmem_limit_bytes=VMEM_LIMIT),
        name="post",
    )(h2d, ya, yb, yc, glu, s5nw, wo, nw, w1, w2, fnw)


def kernel(x, meta, norm_mix_w, norm_mlp_w, w_in, hgrn_lb_logits, hgrn_norm_w, gdn_conv_w, gdn_a_log, gdn_dt_bias, gdn_norm_w, s5_lam_re, s5_lam_im, s5_log_step, s5_b_re, s5_b_im, s5_c_re, s5_c_im, s5_d, s5_glu_w, s5_norm_w, w_out, mlp_w1, mlp_w2, final_norm_w):
    bsz, seq, d_model = x.shape
    assert d_model == D_MODEL
    depth = w_in.shape[0]
    l = N_META + seq
    lp = -(-l // TIME_BLOCK) * TIME_BLOCK
    front = lp - l
    meta_b = jnp.broadcast_to(meta[None].astype(x.dtype), (bsz, N_META, d_model))
    h = jnp.concatenate([jnp.zeros((bsz, front, d_model), x.dtype), meta_b, x], axis=1)
    h2d = h.reshape(bsz * lp, d_model)
    aligned = (lp - seq) % TIME_BLOCK == 0

    lb_table = jnp.cumsum(jax.nn.softmax(hgrn_lb_logits.astype(F32), axis=0), axis=0)
    lb_table = lb_table - lb_table[0]

    o_hg, o_gdn = 4 * HG_W, 4 * HG_W + 4 * GDN_W
    o_s5 = o_gdn + 2 * GDN_HEADS
    for layer in range(depth):
        w = w_in[layer]
        wh = w[:, :o_hg].astype(BF16)
        wg = w[:, o_hg:o_gdn].astype(BF16)
        wab = jnp.pad(w[:, o_gdn:o_s5], ((0, 0), (0, LANES - 2 * GDN_HEADS))).astype(BF16)
        ws = w[:, o_s5:].astype(BF16)
        ph, pg, ps, pab = _inproj(h2d, norm_mix_w[layer].reshape(1, -1), wh, wg, ws, wab)

        ya = _hgrn(ph.reshape(bsz, lp, -1), lb_table[layer].reshape(1, -1),
                   jnp.tile(hgrn_norm_w[layer].astype(F32), HG_HEADS).reshape(1, -1))

        ab = jnp.zeros((2, LANES), F32)
        ab = ab.at[0, :GDN_HEADS].set(-jnp.exp(gdn_a_log[layer].astype(F32)))
        ab = ab.at[1, :GDN_HEADS].set(gdn_dt_bias[layer].astype(F32))
        yb = _gdn(pg.reshape(bsz, lp, -1), pab.reshape(bsz, lp, -1),
                  gdn_conv_w[layer].astype(F32), ab,
                  jnp.tile(gdn_norm_w[layer].astype(F32), GDN_HEADS).reshape(1, -1))

        tables = _s5_tables(s5_lam_re[layer], s5_lam_im[layer], s5_log_step[layer],
                            s5_b_re[layer], s5_b_im[layer], s5_c_re[layer], s5_c_im[layer],
                            s5_d[layer])
        yc = _s5(ps.reshape(bsz, lp, -1), tables)

        h2d = _post(h2d, ya.reshape(bsz * lp, -1), yb.reshape(bsz * lp, -1), yc,
                    s5_glu_w[layer].astype(BF16), s5_norm_w[layer].reshape(1, -1).astype(F32),
                    w_out[layer].astype(BF16), norm_mlp_w[layer].reshape(1, -1).astype(F32),
                    mlp_w1[layer].astype(BF16), mlp_w2[layer].astype(BF16),
                    final_norm_w.reshape(1, -1).astype(F32), final=(layer == depth - 1),
                    drop_blocks=((lp // TIME_BLOCK, (lp - seq) // TIME_BLOCK)
                                 if aligned and layer == depth - 1 else None))
    if aligned:
        return h2d.reshape(bsz, seq, d_model)
    return h2d.reshape(bsz, lp, d_model)[:, lp - seq:]
```

```python
import functools
from typing import NamedTuple, Optional

import jax
import jax.numpy as jnp
from jax import lax
from jax.experimental import pallas as pl
from jax.experimental.pallas import tpu as pltpu

F32 = jnp.float32
BF16 = jnp.bfloat16
HIGHEST = lax.Precision.HIGHEST

EPS = 1e-6
N_META = 16
D_MODEL = 1024
D_FF = 4 * D_MODEL

HG_HEADS = 4
HG_D = 64
HG_W = HG_HEADS * HG_D
HG_CHUNK = 16
HG_UNROLL = 4

GDN_HEADS = 4
GDN_D = 128
GDN_W = GDN_HEADS * GDN_D
GDN_CHUNK = 64
GDN_CONV = 4
GDN_PRE_UNROLL = 4

S5_CH = 256
S5_GROUPS = 16
S5_GROUP_CH = 16
S5_STATE = 64
S5_BLOCK = 8
S5_ROW = S5_BLOCK * S5_CH
S5_NSTATE = S5_GROUPS * S5_STATE

TIME_BLOCK = 512
LANES = 128
VMEM_LIMIT = 56 * 1024 * 1024


def _dot(a, b, precision=None):
    return jnp.dot(a, b, preferred_element_type=F32, precision=precision)


def _dot_nt(a, b, precision=None):
    return lax.dot_general(a, b, (((1,), (1,)), ((), ())),
                           preferred_element_type=F32, precision=precision)


def _dot_tn(a, b, precision=None):
    return lax.dot_general(a, b, (((0,), (0,)), ((), ())),
                           preferred_element_type=F32, precision=precision)


def _rmsnorm(x, w):
    return x * lax.rsqrt(jnp.mean(x * x, axis=-1, keepdims=True) + EPS) * w


def _silu(x):
    half = 0.5 * x
    return half + half * jnp.tanh(half)


def _log1p_exp_neg_abs(x):
    return jnp.log1p(jnp.exp(-jnp.abs(x)))


def _block_masks(n, block):
    r = lax.broadcasted_iota(jnp.int32, (n, n), 0)
    c = lax.broadcasted_iota(jnp.int32, (n, n), 1)
    shift = block.bit_length() - 1
    same = (r >> shift) == (c >> shift)
    tril = jnp.where(same, jnp.where(c <= r, 1.0, 0.0), 0.0).astype(F32)
    ones = jnp.where(same, 1.0, 0.0).astype(F32)
    return tril, ones


class _Residual(NamedTuple):
    rows: jax.Array
    head: jax.Array
    per_seq: Optional[int]

    def total_blocks(self, tm):
        n = self.rows.shape[0] // tm
        return n if self.per_seq is None else n // (self.per_seq - 1) * self.per_seq

    def specs(self, tm):
        return [_rows_spec(tm, self.per_seq), _const_spec(self.head.shape)]


def _rows_spec(tm, per_seq):
    if per_seq is None:
        return pl.BlockSpec((tm, D_MODEL), lambda i: (i, 0))
    return pl.BlockSpec(
        (tm, D_MODEL),
        lambda i: ((i // per_seq) * (per_seq - 1) + jnp.maximum(i % per_seq - 1, 0), 0))


def _load_residual(h_ref, head_ref, per_seq):
    if per_seq is None:
        return h_ref[...]
    return jnp.where(pl.program_id(0) % per_seq == 0, head_ref[...], h_ref[...])


def _inproj_kernel(h_ref, head_ref, nw_ref, wh_ref, wg_ref, ws_ref, wab_ref,
                   ph_ref, pg_ref, ps_ref, pab_ref, *, per_seq):
    hn = _rmsnorm(_load_residual(h_ref, head_ref, per_seq), nw_ref[...]).astype(BF16)
    ph_ref[...] = _dot(hn, wh_ref[...])
    pg_ref[...] = _dot(hn, wg_ref[...])
    su = _dot(hn, ws_ref[...])
    for half in range(S5_CH // LANES):
        ps_ref[half] = su[:, half * LANES:(half + 1) * LANES]
    pab_ref[...] = _dot(hn, wab_ref[...])


def _const_spec(shape):
    return pl.BlockSpec(shape, lambda *_: (0,) * len(shape), pipeline_mode=pl.Buffered(1))


def _inproj(res, nw, wh, wg, ws, wab):
    tm = TIME_BLOCK
    nblk = res.total_blocks(tm)
    m = nblk * tm
    row = lambda w: pl.BlockSpec((tm, w), lambda i: (i, 0))
    return pl.pallas_call(
        functools.partial(_inproj_kernel, per_seq=res.per_seq),
        grid=(nblk,),
        in_specs=res.specs(tm) + [_const_spec((1, D_MODEL)),
                  _const_spec(wh.shape), _const_spec(wg.shape),
                  _const_spec(ws.shape), _const_spec(wab.shape)],
        out_specs=[row(4 * HG_W), row(4 * GDN_W),
                   pl.BlockSpec((S5_CH // LANES, tm, LANES), lambda i: (0, i, 0)), row(LANES)],
        out_shape=[jax.ShapeDtypeStruct((m, 4 * HG_W), F32),
                   jax.ShapeDtypeStruct((m, 4 * GDN_W), F32),
                   jax.ShapeDtypeStruct((S5_CH // LANES, m, LANES), F32),
                   jax.ShapeDtypeStruct((m, LANES), F32)],
        compiler_params=pltpu.CompilerParams(
            dimension_semantics=("arbitrary",), vmem_limit_bytes=VMEM_LIMIT),
        name="inproj",
    )(res.rows, res.head, nw, wh, wg, ws, wab)


def _hgrn_kernel(ph_ref, lb_ref, nw_ref, y_ref,
                 st_ref, qs_s, k_s, v_s, cum_s, qn_s, o_s):
    tb = ph_ref.shape[1]

    @pl.when(pl.program_id(1) == 0)
    def _():
        st_ref[...] = jnp.zeros_like(st_ref)

    q = ph_ref[0, :, 0:HG_W]
    fl = ph_ref[0, :, HG_W:2 * HG_W]
    lb = lb_ref[...]
    ls_pos = jnp.minimum(fl, 0.0) - _log1p_exp_neg_abs(fl)
    ls_neg = ls_pos - fl
    b = jnp.log(lb) + ls_neg
    log_f = jnp.maximum(ls_pos, b) + _log1p_exp_neg_abs(ls_pos - b)
    qs_s[...] = _silu(q)
    k_s[...] = (1.0 - lb) * jnp.exp(ls_neg)
    v_s[...] = ph_ref[0, :, 2 * HG_W:3 * HG_W]

    tril, _ = _block_masks(LANES, HG_CHUNK)
    for g in range(tb // LANES):
        sl = slice(g * LANES, (g + 1) * LANES)
        cum = _dot(tril, log_f[sl], HIGHEST)
        cum_s[sl, :] = cum
        qn_s[sl, :] = qs_s[sl, :] * jnp.exp(cum)

    row = lax.broadcasted_iota(jnp.int32, (HG_CHUNK, LANES), 0)
    er = lax.broadcasted_iota(jnp.int32, (LANES, LANES), 0)
    ec = lax.broadcasted_iota(jnp.int32, (LANES, LANES), 1)
    head_bd = jnp.where((er >> 6) == (ec >> 6), 1.0, 0.0).astype(F32)
    head_bd_bf = head_bd.astype(BF16)

    npair = HG_W // LANES

    def chunks(i, carry):
        items = [(c, p) for c in range(HG_UNROLL) for p in range(npair)]
        r0 = [pl.multiple_of((HG_UNROLL * i + c) * HG_CHUNK, HG_CHUNK) for c in range(HG_UNROLL)]
        rows = [pl.ds(r, HG_CHUNK) for r in r0]
        lanes = lambda p: slice(p * LANES, (p + 1) * LANES)
        m = range(len(items))
        qn = [qn_s[rows[c], lanes(p)] for c, p in items]
        v = [v_s[rows[c], lanes(p)] for c, p in items]
        qs = [qs_s[rows[c], lanes(p)] for c, p in items]
        k = [k_s[rows[c], lanes(p)] for c, p in items]
        cum = [cum_s[rows[c], lanes(p)] for c, p in items]
        last = [cum[j][HG_CHUNK - 1:HG_CHUNK, :] for j in m]
        dec = [jnp.exp(last[j]) for j in m]
        kn = [k[j] * jnp.exp(last[j] - cum[j]) for j in m]
        upd = [_dot_tn(v[j].astype(BF16), kn[j].astype(BF16)) * head_bd for j in m]
        sums = []
        for j in m:
            ps = []
            for s in range(HG_CHUNK):
                e = jnp.exp(jnp.minimum(cum[j] - cum[j][s:s + 1, :], 0.0))
                ps.append((qs[j] * k[j][s:s + 1, :] * e).astype(BF16))
            sums.append(_dot(jnp.concatenate(ps, axis=0), head_bd_bf))
        st = [st_ref[p] for p in range(npair)]
        o = []
        for j, (c, p) in enumerate(items):
            o.append(_dot_nt(qn[j].astype(BF16), st[p].astype(BF16)))
            st[p] = st[p] * dec[j] + upd[j]
        for p in range(npair):
            st_ref[p] = st[p]
        for j, (c, p) in enumerate(items):
            acc = o[j]
            for s in range(HG_CHUNK):
                sc = sums[j][s * HG_CHUNK:(s + 1) * HG_CHUNK]
                acc = acc + jnp.where(row >= s, sc * v[j][s:s + 1, :], 0.0)
            o_s[rows[c], lanes(p)] = acc
        return carry

    lax.fori_loop(0, tb // (HG_CHUNK * HG_UNROLL), chunks, 0)

    o = o_s[...]
    gate = ph_ref[0, :, 3 * HG_W:4 * HG_W]
    r2 = lax.broadcasted_iota(jnp.int32, (HG_W, HG_W), 0)
    c2 = lax.broadcasted_iota(jnp.int32, (HG_W, HG_W), 1)
    mean_mat = jnp.where((r2 >> 6) == (c2 >> 6), 1.0 / HG_D, 0.0).astype(BF16)
    ms = _dot((o * o).astype(BF16), mean_mat)
    y = o * lax.rsqrt(ms + EPS) * nw_ref[...] * _silu(gate)
    y_ref[0] = y.astype(y_ref.dtype)


def _hgrn(ph, lb, nw):
    b, lp, _ = ph.shape
    tb = TIME_BLOCK
    scr = lambda: pltpu.VMEM((tb, HG_W), F32)
    return pl.pallas_call(
        _hgrn_kernel,
        grid=(b, lp // tb),
        in_specs=[pl.BlockSpec((1, tb, 4 * HG_W), lambda i, j: (i, j, 0)),
                  _const_spec((1, HG_W)), _const_spec((1, HG_W))],
        out_specs=pl.BlockSpec((1, tb, HG_W), lambda i, j: (i, j, 0)),
        out_shape=jax.ShapeDtypeStruct((b, lp, HG_W), BF16),
        scratch_shapes=[pltpu.VMEM((HG_W // LANES, LANES, LANES), F32)] + [scr() for _ in range(6)],
        compiler_params=pltpu.CompilerParams(
            dimension_semantics=("arbitrary", "arbitrary"), vmem_limit_bytes=VMEM_LIMIT),
        name="hgrn2",
    )(ph, lb, nw)


def _gdn_kernel(pg_ref, pab_ref, cw_ref, ab_ref, nw_ref, y_ref,
                s_ref, xpad, q_s, k_s, v_s, cumb_s, betab_s, rowt_s,
                lhs_s, add_s, oo_s, o_s):
    tb = pg_ref.shape[1]
    n = GDN_CHUNK
    halo = 8

    @pl.when(pl.program_id(1) == 0)
    def _():
        s_ref[...] = jnp.zeros_like(s_ref)
        xpad[0:halo, :] = jnp.zeros((halo, 3 * GDN_W), F32)

    xpad[halo:halo + tb, :] = pg_ref[0, :, 0:3 * GDN_W]
    acc = cw_ref[GDN_CONV - 1:GDN_CONV, :] * xpad[halo:halo + tb, :]
    for j in range(GDN_CONV - 1):
        off = halo - (GDN_CONV - 1) + j
        acc = acc + cw_ref[j:j + 1, :] * xpad[off:off + tb, :]
    xpad[0:halo, :] = xpad[tb:tb + halo, :]
    c = _silu(acc)

    ones_d = jnp.ones((GDN_D, GDN_D), BF16)
    for h in range(GDN_HEADS):
        ls = slice(h * GDN_D, (h + 1) * GDN_D)
        qh = c[:, h * GDN_D:(h + 1) * GDN_D]
        kh = c[:, GDN_W + h * GDN_D:GDN_W + (h + 1) * GDN_D]
        q_s[:, ls] = qh * lax.rsqrt(_dot((qh * qh).astype(BF16), ones_d) + EPS) * (GDN_D ** -0.5)
        k_s[:, ls] = kh * lax.rsqrt(_dot((kh * kh).astype(BF16), ones_d) + EPS)
    v_s[...] = c[:, 2 * GDN_W:3 * GDN_W]

    ab = pab_ref[0]
    sp = jnp.maximum(ab + ab_ref[1:2, :], 0.0) + _log1p_exp_neg_abs(ab + ab_ref[1:2, :])
    g_col = ab_ref[0:1, :] * sp
    beta = jax.nn.sigmoid(ab)
    tril, _ = _block_masks(LANES, n)
    for g in range(tb // LANES):
        sl = slice(g * LANES, (g + 1) * LANES)
        cum = _dot(tril, g_col[sl], HIGHEST)
        for j in range(LANES // n):
            rowt_s[g * (LANES // n) + j] = cum[j * n:(j + 1) * n, :].T[0:8, :]
        for h in range(GDN_HEADS):
            cumb_s[h, sl, :] = jnp.broadcast_to(cum[:, h:h + 1], (LANES, LANES))
            betab_s[h, sl, :] = jnp.broadcast_to(
                beta[sl][:, GDN_HEADS + h:GDN_HEADS + h + 1], (LANES, LANES))

    r = lax.broadcasted_iota(jnp.int32, (n, n), 0)
    cc = lax.broadcasted_iota(jnp.int32, (n, n), 1)
    incl = cc <= r
    strict = cc < r
    eye = jnp.where(cc == r, 1.0, 0.0).astype(F32)

    def pre_body(i, carry):
        chunks = [GDN_PRE_UNROLL * i + c for c in range(GDN_PRE_UNROLL)]
        items = [(c, h) for c in range(GDN_PRE_UNROLL) for h in range(GDN_HEADS)]
        rows = [pl.ds(pl.multiple_of(ci * n, n), n) for ci in chunks]
        lanes = lambda h: slice(h * GDN_D, (h + 1) * GDN_D)
        qh = [q_s[rows[c], lanes(h)] for c, h in items]
        kh = [k_s[rows[c], lanes(h)] for c, h in items]
        vh = [v_s[rows[c], lanes(h)] for c, h in items]
        cb = [cumb_s[h, rows[c], :] for c, h in items]
        tt = [cb[j][n - 1:n, :] for j in range(len(items))]
        bb = [betab_s[h, rows[c], :] for c, h in items]
        cr = [rowt_s[chunks[c], h:h + 1, :] for c, h in items]
        m = range(len(items))
        eg = [jnp.exp(cb[j]) for j in m]
        kb = [kh[j] * bb[j] for j in m]
        decay = [jnp.exp(jnp.minimum(cb[j][:, 0:n] - cr[j], 0.0)) for j in m]
        both = [_dot_nt(jnp.concatenate([kb[j], qh[j]], axis=0).astype(BF16), kh[j].astype(BF16))
                for j in m]
        a_intra = [jnp.where(incl, both[j][n:2 * n] * decay[j], 0.0).astype(BF16) for j in m]
        p = [jnp.where(strict, both[j][0:n] * decay[j], 0.0) for j in m]
        x = [eye - p[j] for j in m]
        for _ in range(n.bit_length() - 2):
            pb = [p[j].astype(BF16) for j in m]
            p = [_dot(pb[j], pb[j]) for j in m]
            x = [x[j] + _dot(x[j].astype(BF16), p[j].astype(BF16)) for j in m]
        rhs = [jnp.concatenate([vh[j] * bb[j], kb[j] * eg[j]], axis=1).astype(BF16) for j in m]
        uw = [_dot(x[j].astype(BF16), rhs[j]).astype(BF16) for j in m]
        k_dec = [(kh[j] * jnp.exp(tt[j] - cb[j])).astype(BF16) for j in m]
        au_aw = [_dot(a_intra[j], uw[j]) for j in m]
        ku_kw = [_dot_tn(k_dec[j], uw[j]) for j in m]
        for j, (c, h) in enumerate(items):
            ci = chunks[c]
            oo_s[rows[c], lanes(h)] = au_aw[j][:, 0:GDN_D]
            add_s[ci, h] = ku_kw[j][:, 0:GDN_D]
            lhs_s[ci, h, 0:GDN_D, :] = (-ku_kw[j][:, GDN_D:]).astype(BF16)
            lhs_s[ci, h, GDN_D:GDN_D + n, :] = (qh[j] * eg[j] - au_aw[j][:, GDN_D:]).astype(BF16)
        return carry

    lax.fori_loop(0, tb // (GDN_PRE_UNROLL * n), pre_body, 0)

    def scan_body(ci, carry):
        r0 = pl.multiple_of(ci * n, n)
        rows = pl.ds(r0, n)
        hs = range(GDN_HEADS)
        s = [s_ref[h] for h in hs]
        d = [jnp.exp(cumb_s[h, pl.ds(r0 + n - 1, 1), :]) for h in hs]
        res = [_dot(lhs_s[ci, h], s[h].astype(BF16)) for h in hs]
        for h in hs:
            ls = slice(h * GDN_D, (h + 1) * GDN_D)
            o_s[rows, ls] = res[h][GDN_D:, :] + oo_s[rows, ls]
            s_ref[h] = s[h] * d[h] + res[h][0:GDN_D, :] + add_s[ci, h]
        return carry

    lax.fori_loop(0, tb // n, scan_body, 0)

    mean_d = jnp.full((GDN_D, GDN_D), 1.0 / GDN_D, BF16)
    for h in range(GDN_HEADS):
        ls = slice(h * GDN_D, (h + 1) * GDN_D)
        o = o_s[:, ls]
        z = pg_ref[0, :, 3 * GDN_W + h * GDN_D:3 * GDN_W + (h + 1) * GDN_D]
        ms = _dot((o * o).astype(BF16), mean_d)
        y_ref[0, :, ls] = (o * lax.rsqrt(ms + EPS) * nw_ref[:, ls] * _silu(z)).astype(y_ref.dtype)


def _gdn(pg, pab, cw, ab, nw):
    b, lp, _ = pg.shape
    tb = TIME_BLOCK
    nchunk = tb // GDN_CHUNK
    wide = lambda: pltpu.VMEM((tb, GDN_W), F32)
    perhead = lambda: pltpu.VMEM((GDN_HEADS, tb, LANES), F32)
    return pl.pallas_call(
        _gdn_kernel,
        grid=(b, lp // tb),
        in_specs=[pl.BlockSpec((1, tb, 4 * GDN_W), lambda i, j: (i, j, 0)),
                  pl.BlockSpec((1, tb, LANES), lambda i, j: (i, j, 0)),
                  _const_spec(cw.shape), _const_spec(ab.shape), _const_spec(nw.shape)],
        out_specs=pl.BlockSpec((1, tb, GDN_W), lambda i, j: (i, j, 0)),
        out_shape=jax.ShapeDtypeStruct((b, lp, GDN_W), BF16),
        scratch_shapes=[pltpu.VMEM((GDN_HEADS, GDN_D, GDN_D), F32),
                        pltpu.VMEM((tb + 8, 3 * GDN_W), F32),
                        wide(), wide(), wide(), perhead(), perhead(),
                        pltpu.VMEM((nchunk, 8, GDN_CHUNK), F32),
                        pltpu.VMEM((nchunk, GDN_HEADS, GDN_D + GDN_CHUNK, GDN_D), BF16),
                        pltpu.VMEM((nchunk, GDN_HEADS, GDN_D, GDN_D), F32),
                        wide(), wide()],
        compiler_params=pltpu.CompilerParams(
            dimension_semantics=("arbitrary", "arbitrary"), vmem_limit_bytes=VMEM_LIMIT),
        name="gdn",
    )(pg, pab, cw, ab, nw)


def _s5_tables(lam_re, lam_im, log_step, b_re, b_im, c_re, c_im, d):
    g, p, t = S5_GROUPS, S5_STATE, S5_BLOCK
    lre = jnp.minimum(lam_re.astype(F32), -1e-4)
    lim = lam_im.astype(F32)
    dt = jnp.exp(log_step.astype(F32))[:, None]
    mag = jnp.exp(lre * dt)
    ar = mag * jnp.cos(lim * dt)
    ai = mag * jnp.sin(lim * dt)
    den = lre * lre + lim * lim
    nr = ar - 1.0
    ni = ai
    coef_re = ((nr * lre + ni * lim) / den)[..., None]
    coef_im = ((ni * lre - nr * lim) / den)[..., None]
    bbr = coef_re * b_re.astype(F32) - coef_im * b_im.astype(F32)
    bbi = coef_re * b_im.astype(F32) + coef_im * b_re.astype(F32)
    cr = c_re.astype(F32)
    ci = c_im.astype(F32)
    prs, pis = [jnp.ones_like(ar)], [jnp.zeros_like(ai)]
    for _ in range(t):
        prs.append(prs[-1] * ar - pis[-1] * ai)
        pis.append(prs[-2] * ai + pis[-1] * ar)
    pr = jnp.stack(prs)
    pi = jnp.stack(pis)
    wr = pr[:t, :, :, None] * bbr[None] - pi[:t, :, :, None] * bbi[None]
    wi = pr[:t, :, :, None] * bbi[None] + pi[:t, :, :, None] * bbr[None]
    kt = (jnp.einsum("gop,tgpc->tgoc", cr, wr, precision=HIGHEST)
          - jnp.einsum("gop,tgpc->tgoc", ci, wi, precision=HIGHEST))
    tt = jnp.arange(t)
    lag = tt[:, None] - tt[None, :]
    kts = jnp.where((lag >= 0)[:, :, None, None, None], kt[jnp.clip(lag, 0, t - 1)], 0.0)
    def spread(compact, row_group, col_major, col_minor, n_minor, col_group):
        k = jnp.arange(compact.shape[1])
        sel = ((k[:, None] // n_minor == col_major[None, :])
               & (k[:, None] % n_minor == col_minor[None, :])).astype(F32)
        full = jnp.dot(compact, sel, precision=HIGHEST)
        return jnp.where(row_group[:, None] == col_group[None, :], full, 0.0).astype(BF16)

    assert S5_ROW == 2 * S5_NSTATE
    q = jnp.arange(S5_ROW)
    tok_ch_group = (q // S5_GROUP_CH) % g
    state_group = (q // p) % g
    m = spread(kts.transpose(1, 2, 4, 0, 3).reshape(S5_ROW, t * S5_GROUP_CH), tok_ch_group,
               q // S5_CH, q % S5_GROUP_CH, S5_GROUP_CH, tok_ch_group)
    rev = jnp.arange(t - 1, -1, -1)
    win = jnp.stack([wr[rev], wi[rev]], axis=2).transpose(0, 1, 4, 2, 3)
    win = spread(win.reshape(S5_ROW, 2 * p), tok_ch_group,
                 q // S5_NSTATE, q % p, p, state_group)
    er = cr[None] * pr[1:, :, None, :] - ci[None] * pi[1:, :, None, :]
    ei = cr[None] * pi[1:, :, None, :] + ci[None] * pr[1:, :, None, :]
    wout = jnp.stack([er, -ei], axis=0).transpose(0, 2, 4, 1, 3)
    wout = spread(wout.reshape(2 * S5_NSTATE, t * S5_GROUP_CH), state_group,
                  q // S5_CH, q % S5_GROUP_CH, S5_GROUP_CH, tok_ch_group)
    a_r = pr[t].reshape(1, S5_NSTATE)
    a_i = pi[t].reshape(1, S5_NSTATE)
    step = jnp.concatenate([jnp.concatenate([a_r, a_r], axis=1),
                            jnp.concatenate([-a_i, a_i], axis=1)], axis=0)
    d_row = d.astype(F32).reshape(1, S5_CH)
    return m, win, wout, step, d_row


def _gather_blocks(ps_ref, u_scr):
    rows = u_scr.shape[0]
    for s in range(S5_BLOCK):
        for half in range(S5_CH // LANES):
            lo = s * S5_CH + half * LANES
            u_scr[:, lo:lo + LANES] = (
                ps_ref[half, 0, pl.ds(s, rows, stride=S5_BLOCK), :].astype(u_scr.dtype))


def _s5_scan_kernel(ps_ref, win_ref, step_ref, xs_ref, x_scr, bx_scr, u_scr):
    rows = xs_ref.shape[1]

    @pl.when(pl.program_id(1) == 0)
    def _():
        x_scr[...] = jnp.zeros_like(x_scr)

    _gather_blocks(ps_ref, u_scr)
    bx_scr[...] = _dot(u_scr[...], win_ref[...])
    a_same = step_ref[0:1, :]
    a_cross = step_ref[1:2, :]

    def body(r, x):
        xs_ref[0, pl.ds(r, 1), :] = x
        swapped = jnp.concatenate([x[:, S5_NSTATE:], x[:, :S5_NSTATE]], axis=1)
        return a_same * x + a_cross * swapped + bx_scr[pl.ds(r, 1), :]

    x_scr[...] = lax.fori_loop(0, rows, body, x_scr[...])


def _s5_out_kernel(ps_ref, xs_ref, m_ref, wout_ref, d_ref, y_ref, u_scr):
    rows = xs_ref.shape[1]
    _gather_blocks(ps_ref, u_scr)
    y = _dot(u_scr[...], m_ref[...]) + _dot(xs_ref[0].astype(BF16), wout_ref[...])
    for s in range(S5_BLOCK):
        tok = pl.ds(s, rows, stride=S5_BLOCK)
        for half in range(S5_CH // LANES):
            lo = s * S5_CH + half * LANES
            y_ref[half, 0, tok, :] = (y[:, lo:lo + LANES]
                                      + d_ref[:, half * LANES:(half + 1) * LANES] * ps_ref[half, 0, tok, :])


def _largest_divisor(n, candidates):
    for cand in candidates:
        if n % cand == 0:
            return cand
    return n


def _s5(ps, tables):
    m_w, win, wout, step, d_row = tables
    halves, b, lp, _ = ps.shape
    nb = lp // S5_BLOCK
    rows = _largest_divisor(nb, (704, 512, 256, 128, 64, 32, 16, 8))
    tok_spec = lambda r: pl.BlockSpec((halves, 1, r * S5_BLOCK, LANES), lambda i, j: (0, i, j, 0))
    blk_spec = lambda r: pl.BlockSpec((1, r, 2 * S5_NSTATE), lambda i, j: (i, j, 0))
    params = pltpu.CompilerParams(
        dimension_semantics=("arbitrary", "arbitrary"), vmem_limit_bytes=VMEM_LIMIT)
    xs = pl.pallas_call(
        _s5_scan_kernel,
        grid=(b, nb // rows),
        in_specs=[tok_spec(rows), _const_spec(win.shape), _const_spec(step.shape)],
        out_specs=blk_spec(rows),
        out_shape=jax.ShapeDtypeStruct((b, nb, 2 * S5_NSTATE), F32),
        scratch_shapes=[pltpu.VMEM((1, 2 * S5_NSTATE), F32),
                        pltpu.VMEM((rows, 2 * S5_NSTATE), F32),
                        pltpu.VMEM((rows, S5_ROW), BF16)],
        compiler_params=params,
        name="s5_scan",
    )(ps, win, step)
    rows = _largest_divisor(nb, (352, 256, 128, 64, 32, 16, 8))
    return pl.pallas_call(
        _s5_out_kernel,
        grid=(b, nb // rows),
        in_specs=[tok_spec(rows), blk_spec(rows), _const_spec(m_w.shape),
                  _const_spec(wout.shape), _const_spec(d_row.shape)],
        out_specs=tok_spec(rows),
        out_shape=jax.ShapeDtypeStruct((halves, b, lp, LANES), F32),
        scratch_shapes=[pltpu.VMEM((rows, S5_ROW), BF16)],
        compiler_params=params,
        name="s5_out",
    )(ps, xs, m_w, wout, d_row)


def _post_kernel(h_ref, head_ref, ya_ref, yb_ref, yc_ref, glu_ref, s5nw_ref, wo_ref, nw_ref,
                 w1_ref, w2_ref, fnw_ref, o_ref, *, final, per_seq):
    yc = jnp.concatenate([yc_ref[half] for half in range(S5_CH // LANES)], axis=1)
    zg = 0.5 * yc * (1.0 + jnp.tanh(0.7978845608028654 * (yc + 0.044715 * (yc * yc * yc))))
    out = zg * jax.nn.sigmoid(_dot(zg.astype(BF16), glu_ref[...]))
    y_c = _rmsnorm(out, s5nw_ref[...]).astype(BF16)
    h1 = _load_residual(h_ref, head_ref, per_seq)
    h1 = h1 + _dot(ya_ref[...], wo_ref[0:HG_W, :])
    h1 = h1 + _dot(yb_ref[...], wo_ref[HG_W:HG_W + GDN_W, :])
    h1 = h1 + _dot(y_c, wo_ref[HG_W + GDN_W:, :])
    hn = _rmsnorm(h1, nw_ref[...]).astype(BF16)
    mlp = None
    for j in range(D_FF // D_MODEL):
        sl = slice(j * D_MODEL, (j + 1) * D_MODEL)
        a = _dot(hn, w1_ref[:, sl])
        part = _dot(jnp.square(jnp.maximum(a, 0.0)).astype(BF16), w2_ref[sl, :])
        mlp = part if mlp is None else mlp + part
    acc = h1 + mlp
    if final:
        acc = _rmsnorm(acc, fnw_ref[...])
    o_ref[...] = acc


def _post(res, ya, yb, yc, glu, s5nw, wo, nw, w1, w2, fnw, final, drop_head_of=None):
    tm = TIME_BLOCK
    nblk = res.total_blocks(tm)
    m = nblk * tm
    row = lambda w: pl.BlockSpec((tm, w), lambda i: (i, 0))
    out_spec, out_rows = row(D_MODEL), m
    if drop_head_of is not None:
        out_rows = m // drop_head_of * (drop_head_of - 1)
        out_spec = _rows_spec(tm, drop_head_of)
    return pl.pallas_call(
        functools.partial(_post_kernel, final=final, per_seq=res.per_seq),
        grid=(nblk,),
        in_specs=res.specs(tm) + [row(HG_W), row(GDN_W),
                  pl.BlockSpec((S5_CH // LANES, tm, LANES), lambda i: (0, i, 0)),
                  _const_spec(glu.shape), _const_spec(s5nw.shape), _const_spec(wo.shape),
                  _const_spec(nw.shape), _const_spec(w1.shape), _const_spec(w2.shape),
                  _const_spec(fnw.shape)],
        out_specs=out_spec,
        out_shape=jax.ShapeDtypeStruct((out_rows, D_MODEL), F32),
        compiler_params=pltpu.CompilerParams(
            dimension_semantics=("arbitrary",), vmem_limit_bytes=VMEM_LIMIT),
        name="post",
    )(res.rows, res.head, ya, yb, yc, glu, s5nw, wo, nw, w1, w2, fnw)


def kernel(x, meta, norm_mix_w, norm_mlp_w, w_in, hgrn_lb_logits, hgrn_norm_w, gdn_conv_w, gdn_a_log, gdn_dt_bias, gdn_norm_w, s5_lam_re, s5_lam_im, s5_log_step, s5_b_re, s5_b_im, s5_c_re, s5_c_im, s5_d, s5_glu_w, s5_norm_w, w_out, mlp_w1, mlp_w2, final_norm_w):
    bsz, seq, d_model = x.shape
    assert d_model == D_MODEL
    depth = w_in.shape[0]
    l = N_META + seq
    lp = -(-l // TIME_BLOCK) * TIME_BLOCK
    front = lp - l
    head = jnp.concatenate([jnp.zeros((front, d_model), x.dtype), meta.astype(x.dtype)], axis=0)
    no_head = jnp.zeros((8, d_model), x.dtype)
    per_seq = lp // TIME_BLOCK
    head_is_block = lp - seq == TIME_BLOCK
    if head_is_block:
        res = _Residual(x.reshape(bsz * seq, d_model), head, per_seq)
    else:
        h = jnp.concatenate([jnp.broadcast_to(head[None], (bsz,) + head.shape), x], axis=1)
        res = _Residual(h.reshape(bsz * lp, d_model), no_head, None)

    lb_table = jnp.cumsum(jax.nn.softmax(hgrn_lb_logits.astype(F32), axis=0), axis=0)
    lb_table = lb_table - lb_table[0]

    o_hg, o_gdn = 4 * HG_W, 4 * HG_W + 4 * GDN_W
    o_s5 = o_gdn + 2 * GDN_HEADS
    for layer in range(depth):
        w = w_in[layer]
        wh = w[:, :o_hg].astype(BF16)
        wg = w[:, o_hg:o_gdn].astype(BF16)
        wab = jnp.pad(w[:, o_gdn:o_s5], ((0, 0), (0, LANES - 2 * GDN_HEADS))).astype(BF16)
        ws = w[:, o_s5:].astype(BF16)
        ph, pg, ps, pab = _inproj(res, norm_mix_w[layer].reshape(1, -1), wh, wg, ws, wab)

        ya = _hgrn(ph.reshape(bsz, lp, -1), lb_table[layer].reshape(1, -1),
                   jnp.tile(hgrn_norm_w[layer].astype(F32), HG_HEADS).reshape(1, -1))

        ab = jnp.zeros((2, LANES), F32)
        ab = ab.at[0, :GDN_HEADS].set(-jnp.exp(gdn_a_log[layer].astype(F32)))
        ab = ab.at[1, :GDN_HEADS].set(gdn_dt_bias[layer].astype(F32))
        yb = _gdn(pg.reshape(bsz, lp, -1), pab.reshape(bsz, lp, -1),
                  gdn_conv_w[layer].astype(F32), ab,
                  jnp.tile(gdn_norm_w[layer].astype(F32), GDN_HEADS).reshape(1, -1))

        tables = _s5_tables(s5_lam_re[layer], s5_lam_im[layer], s5_log_step[layer],
                            s5_b_re[layer], s5_b_im[layer], s5_c_re[layer], s5_c_im[layer],
                            s5_d[layer])
        yc = _s5(ps.reshape(-1, bsz, lp, LANES), tables).reshape(-1, bsz * lp, LANES)

        last = layer == depth - 1
        h2d = _post(res, ya.reshape(bsz * lp, -1), yb.reshape(bsz * lp, -1), yc,
                    s5_glu_w[layer].astype(BF16), s5_norm_w[layer].reshape(1, -1).astype(F32),
                    w_out[layer].astype(BF16), norm_mlp_w[layer].reshape(1, -1).astype(F32),
                    mlp_w1[layer].astype(BF16), mlp_w2[layer].astype(BF16),
                    final_norm_w.reshape(1, -1).astype(F32), final=last,
                    drop_head_of=per_seq if last and head_is_block else None)
        res = _Residual(h2d, no_head, None)
    if head_is_block:
        return h2d.reshape(bsz, seq, d_model)
    return h2d.reshape(bsz, lp, d_model)[:, lp - seq:]
```

```python
import functools
from typing import NamedTuple, Optional

import jax
import jax.numpy as jnp
from jax import lax
from jax.experimental import pallas as pl
from jax.experimental.pallas import tpu as pltpu

F32 = jnp.float32
BF16 = jnp.bfloat16
HIGHEST = lax.Precision.HIGHEST

EPS = 1e-6
N_META = 16
D_MODEL = 1024
D_FF = 4 * D_MODEL

HG_HEADS = 4
HG_D = 64
HG_W = HG_HEADS * HG_D
HG_CHUNK = 16
HG_UNROLL = 4

GDN_HEADS = 4
GDN_D = 128
GDN_W = GDN_HEADS * GDN_D
GDN_CHUNK = 64
GDN_CONV = 4
GDN_PRE_UNROLL = 4

S5_CH = 256
S5_GROUPS = 16
S5_GROUP_CH = 16
S5_STATE = 64
S5_BLOCK = 8
S5_ROW = S5_BLOCK * S5_CH
S5_NSTATE = S5_GROUPS * S5_STATE

TIME_BLOCK = 512
LANES = 128
VMEM_LIMIT = 56 * 1024 * 1024


def _dot(a, b, precision=None):
    return jnp.dot(a, b, preferred_element_type=F32, precision=precision)


def _dot_nt(a, b, precision=None):
    return lax.dot_general(a, b, (((1,), (1,)), ((), ())),
                           preferred_element_type=F32, precision=precision)


def _dot_tn(a, b, precision=None):
    return lax.dot_general(a, b, (((0,), (0,)), ((), ())),
                           preferred_element_type=F32, precision=precision)


def _rmsnorm(x, w):
    return x * lax.rsqrt(jnp.mean(x * x, axis=-1, keepdims=True) + EPS) * w


def _silu(x):
    half = 0.5 * x
    return half + half * jnp.tanh(half)


def _log1p_exp_neg_abs(x):
    return jnp.log1p(jnp.exp(-jnp.abs(x)))


def _block_masks(n, block):
    r = lax.broadcasted_iota(jnp.int32, (n, n), 0)
    c = lax.broadcasted_iota(jnp.int32, (n, n), 1)
    shift = block.bit_length() - 1
    same = (r >> shift) == (c >> shift)
    tril = jnp.where(same, jnp.where(c <= r, 1.0, 0.0), 0.0).astype(F32)
    ones = jnp.where(same, 1.0, 0.0).astype(F32)
    return tril, ones


class _Residual(NamedTuple):
    rows: jax.Array
    head: jax.Array
    per_seq: Optional[int]

    def total_blocks(self, tm):
        n = self.rows.shape[0] // tm
        return n if self.per_seq is None else n // (self.per_seq - 1) * self.per_seq

    def specs(self, tm):
        return [_rows_spec(tm, self.per_seq), _const_spec(self.head.shape)]


def _rows_spec(tm, per_seq):
    if per_seq is None:
        return pl.BlockSpec((tm, D_MODEL), lambda i: (i, 0))
    return pl.BlockSpec(
        (tm, D_MODEL),
        lambda i: ((i // per_seq) * (per_seq - 1) + jnp.maximum(i % per_seq - 1, 0), 0))


def _load_residual(h_ref, head_ref, per_seq):
    if per_seq is None:
        return h_ref[...]
    return jnp.where(pl.program_id(0) % per_seq == 0, head_ref[...], h_ref[...])


def _inproj_kernel(h_ref, head_ref, nw_ref, cw_ref, wh_ref, wg_ref, ws_ref, wab_ref,
                   ph_ref, pg_ref, ps_ref, pab_ref, xpad, *, per_seq, seq_blocks):
    tm = h_ref.shape[0]
    halo = 8
    @pl.when(pl.program_id(0) % seq_blocks == 0)
    def _():
        xpad[0:halo, :] = jnp.zeros((halo, 3 * GDN_W), F32)

    hn = _rmsnorm(_load_residual(h_ref, head_ref, per_seq), nw_ref[...]).astype(BF16)
    for grp in range(3):
        cols = slice(grp * GDN_W, (grp + 1) * GDN_W)
        xpad[halo:halo + tm, cols] = _dot(hn, wg_ref[:, cols])
    pg_ref[:, 3 * GDN_W:] = _silu(_dot(hn, wg_ref[:, 3 * GDN_W:])).astype(pg_ref.dtype)
    ph_ref[...] = _dot(hn, wh_ref[...])
    su = _dot(hn, ws_ref[...])
    for half in range(S5_CH // LANES):
        ps_ref[half] = su[:, half * LANES:(half + 1) * LANES]
    pab_ref[...] = _dot(hn, wab_ref[...])
    acc = cw_ref[GDN_CONV - 1:GDN_CONV, :] * xpad[halo:halo + tm, :]
    for j in range(GDN_CONV - 1):
        off = halo - (GDN_CONV - 1) + j
        acc = acc + cw_ref[j:j + 1, :] * xpad[off:off + tm, :]
    xpad[0:halo, :] = xpad[tm:tm + halo, :]
    pg_ref[:, 0:3 * GDN_W] = _silu(acc).astype(pg_ref.dtype)


def _const_spec(shape):
    return pl.BlockSpec(shape, lambda *_: (0,) * len(shape), pipeline_mode=pl.Buffered(1))


def _inproj(res, seq_blocks, nw, cw, wh, wg, ws, wab):
    tm = TIME_BLOCK
    nblk = res.total_blocks(tm)
    m = nblk * tm
    row = lambda w: pl.BlockSpec((tm, w), lambda i: (i, 0))
    return pl.pallas_call(
        functools.partial(_inproj_kernel, per_seq=res.per_seq, seq_blocks=seq_blocks),
        grid=(nblk,),
        in_specs=res.specs(tm) + [_const_spec((1, D_MODEL)), _const_spec(cw.shape),
                  _const_spec(wh.shape), _const_spec(wg.shape),
                  _const_spec(ws.shape), _const_spec(wab.shape)],
        out_specs=[row(4 * HG_W), row(4 * GDN_W),
                   pl.BlockSpec((S5_CH // LANES, tm, LANES), lambda i: (0, i, 0)), row(LANES)],
        out_shape=[jax.ShapeDtypeStruct((m, 4 * HG_W), F32),
                   jax.ShapeDtypeStruct((m, 4 * GDN_W), BF16),
                   jax.ShapeDtypeStruct((S5_CH // LANES, m, LANES), F32),
                   jax.ShapeDtypeStruct((m, LANES), F32)],
        scratch_shapes=[pltpu.VMEM((tm + 8, 3 * GDN_W), F32)],
        compiler_params=pltpu.CompilerParams(
            dimension_semantics=("arbitrary",), vmem_limit_bytes=VMEM_LIMIT),
        name="inproj",
    )(res.rows, res.head, nw, cw, wh, wg, ws, wab)


def _hgrn_kernel(ph_ref, lb_ref, nw_ref, y_ref,
                 st_ref, qs_s, k_s, v_s, cum_s, qn_s, o_s):
    tb = ph_ref.shape[1]

    @pl.when(pl.program_id(1) == 0)
    def _():
        st_ref[...] = jnp.zeros_like(st_ref)

    q = ph_ref[0, :, 0:HG_W]
    fl = ph_ref[0, :, HG_W:2 * HG_W]
    lb = lb_ref[...]
    ls_pos = jnp.minimum(fl, 0.0) - _log1p_exp_neg_abs(fl)
    ls_neg = ls_pos - fl
    b = jnp.log(lb) + ls_neg
    log_f = jnp.maximum(ls_pos, b) + _log1p_exp_neg_abs(ls_pos - b)
    qs_s[...] = _silu(q)
    k_s[...] = (1.0 - lb) * jnp.exp(ls_neg)
    v_s[...] = ph_ref[0, :, 2 * HG_W:3 * HG_W]

    tril, _ = _block_masks(LANES, HG_CHUNK)
    for g in range(tb // LANES):
        sl = slice(g * LANES, (g + 1) * LANES)
        cum = _dot(tril, log_f[sl], HIGHEST)
        cum_s[sl, :] = cum
        qn_s[sl, :] = qs_s[sl, :] * jnp.exp(cum)

    half = HG_CHUNK // 2
    row = lax.broadcasted_iota(jnp.int32, (half, LANES), 0)
    er = lax.broadcasted_iota(jnp.int32, (LANES, LANES), 0)
    ec = lax.broadcasted_iota(jnp.int32, (LANES, LANES), 1)
    head_bd = jnp.where((er >> 6) == (ec >> 6), 1.0, 0.0).astype(F32)
    head_bd_bf = head_bd.astype(BF16)

    npair = HG_W // LANES

    def chunks(i, carry):
        items = [(c, p) for c in range(HG_UNROLL) for p in range(npair)]
        r0 = [pl.multiple_of((HG_UNROLL * i + c) * HG_CHUNK, HG_CHUNK) for c in range(HG_UNROLL)]
        rows = [pl.ds(r, HG_CHUNK) for r in r0]
        lanes = lambda p: slice(p * LANES, (p + 1) * LANES)
        m = range(len(items))
        qn = [qn_s[rows[c], lanes(p)] for c, p in items]
        v = [v_s[rows[c], lanes(p)] for c, p in items]
        qs = [qs_s[rows[c], lanes(p)] for c, p in items]
        k = [k_s[rows[c], lanes(p)] for c, p in items]
        cum = [cum_s[rows[c], lanes(p)] for c, p in items]
        last = [cum[j][HG_CHUNK - 1:HG_CHUNK, :] for j in m]
        dec = [jnp.exp(last[j]) for j in m]
        kn = [k[j] * jnp.exp(last[j] - cum[j]) for j in m]
        upd = [_dot_tn(v[j].astype(BF16), kn[j].astype(BF16)) * head_bd for j in m]
        sums = []
        for j in m:
            ps = []
            for s in range(HG_CHUNK):
                lo = 0 if s < half else half
                e = jnp.exp(jnp.minimum(cum[j][lo:] - cum[j][s:s + 1, :], 0.0))
                prod = qs[j][lo:] * k[j][s:s + 1, :] * e
                if lo:
                    prod = jnp.concatenate([jnp.zeros((lo, LANES), F32), prod], axis=0)
                ps.append(prod.astype(BF16))
            sums.append(_dot(jnp.concatenate(ps, axis=0), head_bd_bf))
        st = [st_ref[p] for p in range(npair)]
        o = []
        for j, (c, p) in enumerate(items):
            o.append(_dot_nt(qn[j].astype(BF16), st[p].astype(BF16)))
            st[p] = st[p] * dec[j] + upd[j]
        for p in range(npair):
            st_ref[p] = st[p]
        for j, (c, p) in enumerate(items):
            top, bot = o[j][0:half], o[j][half:]
            for s in range(HG_CHUNK):
                sc = sums[j][s * HG_CHUNK:(s + 1) * HG_CHUNK]
                vs = v[j][s:s + 1, :]
                if s < half:
                    top = top + jnp.where(row >= s, sc[0:half] * vs, 0.0)
                    bot = bot + sc[half:] * vs
                else:
                    bot = bot + jnp.where(row >= s - half, sc[half:] * vs, 0.0)
            o_s[rows[c], lanes(p)] = jnp.concatenate([top, bot], axis=0)
        return carry

    lax.fori_loop(0, tb // (HG_CHUNK * HG_UNROLL), chunks, 0)

    o = o_s[...]
    gate = ph_ref[0, :, 3 * HG_W:4 * HG_W]
    r2 = lax.broadcasted_iota(jnp.int32, (HG_W, HG_W), 0)
    c2 = lax.broadcasted_iota(jnp.int32, (HG_W, HG_W), 1)
    mean_mat = jnp.where((r2 >> 6) == (c2 >> 6), 1.0 / HG_D, 0.0).astype(BF16)
    ms = _dot((o * o).astype(BF16), mean_mat)
    y = o * lax.rsqrt(ms + EPS) * nw_ref[...] * _silu(gate)
    y_ref[0] = y.astype(y_ref.dtype)


def _hgrn(ph, lb, nw):
    b, lp, _ = ph.shape
    tb = TIME_BLOCK
    scr = lambda: pltpu.VMEM((tb, HG_W), F32)
    return pl.pallas_call(
        _hgrn_kernel,
        grid=(b, lp // tb),
        in_specs=[pl.BlockSpec((1, tb, 4 * HG_W), lambda i, j: (i, j, 0)),
                  _const_spec((1, HG_W)), _const_spec((1, HG_W))],
        out_specs=pl.BlockSpec((1, tb, HG_W), lambda i, j: (i, j, 0)),
        out_shape=jax.ShapeDtypeStruct((b, lp, HG_W), BF16),
        scratch_shapes=[pltpu.VMEM((HG_W // LANES, LANES, LANES), F32)] + [scr() for _ in range(6)],
        compiler_params=pltpu.CompilerParams(
            dimension_semantics=("arbitrary", "arbitrary"), vmem_limit_bytes=VMEM_LIMIT),
        name="hgrn2",
    )(ph, lb, nw)


def _gdn_kernel(pg_ref, pab_ref, ab_ref, nw_ref, y_ref,
                s_ref, q_s, k_s, cumb_s, betab_s, rowt_s,
                lhs_s, add_s, oo_s, o_s):
    tb = pg_ref.shape[1]
    n = GDN_CHUNK

    @pl.when(pl.program_id(1) == 0)
    def _():
        s_ref[...] = jnp.zeros_like(s_ref)

    ones_d = jnp.ones((GDN_D, GDN_D), BF16)
    for h in range(GDN_HEADS):
        ls = slice(h * GDN_D, (h + 1) * GDN_D)
        qb = pg_ref[0, :, h * GDN_D:(h + 1) * GDN_D]
        kb = pg_ref[0, :, GDN_W + h * GDN_D:GDN_W + (h + 1) * GDN_D]
        qh = qb.astype(F32)
        kh = kb.astype(F32)
        q_s[:, ls] = qh * lax.rsqrt(_dot(qb * qb, ones_d) + EPS) * (GDN_D ** -0.5)
        k_s[:, ls] = kh * lax.rsqrt(_dot(kb * kb, ones_d) + EPS)

    ab = pab_ref[0]
    sp = jnp.maximum(ab + ab_ref[1:2, :], 0.0) + _log1p_exp_neg_abs(ab + ab_ref[1:2, :])
    g_col = ab_ref[0:1, :] * sp
    beta = jax.nn.sigmoid(ab)
    tril, _ = _block_masks(LANES, n)
    for g in range(tb // LANES):
        sl = slice(g * LANES, (g + 1) * LANES)
        cum = _dot(tril, g_col[sl], HIGHEST)
        for j in range(LANES // n):
            rowt_s[g * (LANES // n) + j] = cum[j * n:(j + 1) * n, :].T[0:8, :]
        for h in range(GDN_HEADS):
            cumb_s[h, sl, :] = jnp.broadcast_to(cum[:, h:h + 1], (LANES, LANES))
            betab_s[h, sl, :] = jnp.broadcast_to(
                beta[sl][:, GDN_HEADS + h:GDN_HEADS + h + 1], (LANES, LANES))

    r = lax.broadcasted_iota(jnp.int32, (n, n), 0)
    cc = lax.broadcasted_iota(jnp.int32, (n, n), 1)
    incl = cc <= r
    strict = cc < r
    eye = jnp.where(cc == r, 1.0, 0.0).astype(F32)

    def pre_body(i, carry):
        chunks = [GDN_PRE_UNROLL * i + c for c in range(GDN_PRE_UNROLL)]
        items = [(c, h) for c in range(GDN_PRE_UNROLL) for h in range(GDN_HEADS)]
        rows = [pl.ds(pl.multiple_of(ci * n, n), n) for ci in chunks]
        lanes = lambda h: slice(h * GDN_D, (h + 1) * GDN_D)
        qh = [q_s[rows[c], lanes(h)] for c, h in items]
        kh = [k_s[rows[c], lanes(h)] for c, h in items]
        vh = [pg_ref[0, rows[c], 2 * GDN_W + h * GDN_D:2 * GDN_W + (h + 1) * GDN_D].astype(F32)
              for c, h in items]
        cb = [cumb_s[h, rows[c], :] for c, h in items]
        tt = [cb[j][n - 1:n, :] for j in range(len(items))]
        bb = [betab_s[h, rows[c], :] for c, h in items]
        cr = [rowt_s[chunks[c], h:h + 1, :] for c, h in items]
        m = range(len(items))
        eg = [jnp.exp(cb[j]) for j in m]
        kb = [kh[j] * bb[j] for j in m]
        decay = [jnp.exp(jnp.minimum(cb[j][:, 0:n] - cr[j], 0.0)) for j in m]
        both = [_dot_nt(jnp.concatenate([kb[j], qh[j]], axis=0).astype(BF16), kh[j].astype(BF16))
                for j in m]
        a_intra = [jnp.where(incl, both[j][n:2 * n] * decay[j], 0.0).astype(BF16) for j in m]
        p = [jnp.where(strict, both[j][0:n] * decay[j], 0.0) for j in m]
        x = [eye - p[j] for j in m]
        for _ in range(n.bit_length() - 2):
            pb = [p[j].astype(BF16) for j in m]
            p = [_dot(pb[j], pb[j]) for j in m]
            x = [x[j] + _dot(x[j].astype(BF16), p[j].astype(BF16)) for j in m]
        rhs = [jnp.concatenate([vh[j] * bb[j], kb[j] * eg[j]], axis=1).astype(BF16) for j in m]
        uw = [_dot(x[j].astype(BF16), rhs[j]).astype(BF16) for j in m]
        k_dec = [(kh[j] * jnp.exp(tt[j] - cb[j])).astype(BF16) for j in m]
        au_aw = [_dot(a_intra[j], uw[j]) for j in m]
        ku_kw = [_dot_tn(k_dec[j], uw[j]) for j in m]
        for j, (c, h) in enumerate(items):
            ci = chunks[c]
            oo_s[rows[c], lanes(h)] = au_aw[j][:, 0:GDN_D]
            add_s[ci, h] = ku_kw[j][:, 0:GDN_D]
            lhs_s[ci, h, 0:GDN_D, :] = (-ku_kw[j][:, GDN_D:]).astype(BF16)
            lhs_s[ci, h, GDN_D:GDN_D + n, :] = (qh[j] * eg[j] - au_aw[j][:, GDN_D:]).astype(BF16)
        return carry

    lax.fori_loop(0, tb // (GDN_PRE_UNROLL * n), pre_body, 0)

    def scan_body(ci, carry):
        r0 = pl.multiple_of(ci * n, n)
        rows = pl.ds(r0, n)
        hs = range(GDN_HEADS)
        s = [s_ref[h] for h in hs]
        d = [jnp.exp(cumb_s[h, pl.ds(r0 + n - 1, 1), :]) for h in hs]
        res = [_dot(lhs_s[ci, h], s[h].astype(BF16)) for h in hs]
        for h in hs:
            ls = slice(h * GDN_D, (h + 1) * GDN_D)
            o_s[rows, ls] = res[h][GDN_D:, :] + oo_s[rows, ls]
            s_ref[h] = s[h] * d[h] + res[h][0:GDN_D, :] + add_s[ci, h]
        return carry

    lax.fori_loop(0, tb // n, scan_body, 0)

    mean_d = jnp.full((GDN_D, GDN_D), 1.0 / GDN_D, BF16)
    for h in range(GDN_HEADS):
        ls = slice(h * GDN_D, (h + 1) * GDN_D)
        o = o_s[:, ls]
        gate = pg_ref[0, :, 3 * GDN_W + h * GDN_D:3 * GDN_W + (h + 1) * GDN_D].astype(F32)
        ms = _dot((o * o).astype(BF16), mean_d)
        y_ref[0, :, ls] = (o * lax.rsqrt(ms + EPS) * nw_ref[:, ls] * gate).astype(y_ref.dtype)


def _gdn(pg, pab, ab, nw):
    b, lp, _ = pg.shape
    tb = TIME_BLOCK
    nchunk = tb // GDN_CHUNK
    wide = lambda: pltpu.VMEM((tb, GDN_W), F32)
    perhead = lambda: pltpu.VMEM((GDN_HEADS, tb, LANES), F32)
    return pl.pallas_call(
        _gdn_kernel,
        grid=(b, lp // tb),
        in_specs=[pl.BlockSpec((1, tb, 4 * GDN_W), lambda i, j: (i, j, 0)),
                  pl.BlockSpec((1, tb, LANES), lambda i, j: (i, j, 0)),
                  _const_spec(ab.shape), _const_spec(nw.shape)],
        out_specs=pl.BlockSpec((1, tb, GDN_W), lambda i, j: (i, j, 0)),
        out_shape=jax.ShapeDtypeStruct((b, lp, GDN_W), BF16),
        scratch_shapes=[pltpu.VMEM((GDN_HEADS, GDN_D, GDN_D), F32),
                        wide(), wide(), perhead(), perhead(),
                        pltpu.VMEM((nchunk, 8, GDN_CHUNK), F32),
                        pltpu.VMEM((nchunk, GDN_HEADS, GDN_D + GDN_CHUNK, GDN_D), BF16),
                        pltpu.VMEM((nchunk, GDN_HEADS, GDN_D, GDN_D), F32),
                        wide(), wide()],
        compiler_params=pltpu.CompilerParams(
            dimension_semantics=("arbitrary", "arbitrary"), vmem_limit_bytes=VMEM_LIMIT),
        name="gdn",
    )(pg, pab, ab, nw)


def _s5_tables(lam_re, lam_im, log_step, b_re, b_im, c_re, c_im, d):
    g, p, t = S5_GROUPS, S5_STATE, S5_BLOCK
    lre = jnp.minimum(lam_re.astype(F32), -1e-4)
    lim = lam_im.astype(F32)
    dt = jnp.exp(log_step.astype(F32))[:, None]
    mag = jnp.exp(lre * dt)
    ar = mag * jnp.cos(lim * dt)
    ai = mag * jnp.sin(lim * dt)
    den = lre * lre + lim * lim
    nr = ar - 1.0
    ni = ai
    coef_re = ((nr * lre + ni * lim) / den)[..., None]
    coef_im = ((ni * lre - nr * lim) / den)[..., None]
    bbr = coef_re * b_re.astype(F32) - coef_im * b_im.astype(F32)
    bbi = coef_re * b_im.astype(F32) + coef_im * b_re.astype(F32)
    cr = c_re.astype(F32)
    ci = c_im.astype(F32)
    prs, pis = [jnp.ones_like(ar)], [jnp.zeros_like(ai)]
    for _ in range(t):
        prs.append(prs[-1] * ar - pis[-1] * ai)
        pis.append(prs[-2] * ai + pis[-1] * ar)
    pr = jnp.stack(prs)
    pi = jnp.stack(pis)
    wr = pr[:t, :, :, None] * bbr[None] - pi[:t, :, :, None] * bbi[None]
    wi = pr[:t, :, :, None] * bbi[None] + pi[:t, :, :, None] * bbr[None]
    kt = (jnp.einsum("gop,tgpc->tgoc", cr, wr, precision=HIGHEST)
          - jnp.einsum("gop,tgpc->tgoc", ci, wi, precision=HIGHEST))
    tt = jnp.arange(t)
    lag = tt[:, None] - tt[None, :]
    kts = jnp.where((lag >= 0)[:, :, None, None, None], kt[jnp.clip(lag, 0, t - 1)], 0.0)
    def spread(compact, row_group, col_major, col_minor, n_minor, col_group):
        k = jnp.arange(compact.shape[1])
        sel = ((k[:, None] // n_minor == col_major[None, :])
               & (k[:, None] % n_minor == col_minor[None, :])).astype(F32)
        full = jnp.dot(compact, sel, precision=HIGHEST)
        return jnp.where(row_group[:, None] == col_group[None, :], full, 0.0).astype(BF16)

    assert S5_ROW == 2 * S5_NSTATE
    q = jnp.arange(S5_ROW)
    tok_ch_group = (q // S5_GROUP_CH) % g
    state_group = (q // p) % g
    m = spread(kts.transpose(1, 2, 4, 0, 3).reshape(S5_ROW, t * S5_GROUP_CH), tok_ch_group,
               q // S5_CH, q % S5_GROUP_CH, S5_GROUP_CH, tok_ch_group)
    rev = jnp.arange(t - 1, -1, -1)
    win = jnp.stack([wr[rev], wi[rev]], axis=2).transpose(0, 1, 4, 2, 3)
    win = spread(win.reshape(S5_ROW, 2 * p), tok_ch_group,
                 q // S5_NSTATE, q % p, p, state_group)
    er = cr[None] * pr[1:, :, None, :] - ci[None] * pi[1:, :, None, :]
    ei = cr[None] * pi[1:, :, None, :] + ci[None] * pr[1:, :, None, :]
    wout = jnp.stack([er, -ei], axis=0).transpose(0, 2, 4, 1, 3)
    wout = spread(wout.reshape(2 * S5_NSTATE, t * S5_GROUP_CH), state_group,
                  q // S5_CH, q % S5_GROUP_CH, S5_GROUP_CH, tok_ch_group)
    a_r = pr[t].reshape(1, S5_NSTATE)
    a_i = pi[t].reshape(1, S5_NSTATE)
    step = jnp.concatenate([jnp.concatenate([a_r, a_r], axis=1),
                            jnp.concatenate([-a_i, a_i], axis=1)], axis=0)
    d_row = d.astype(F32).reshape(1, S5_CH)
    return m, win, wout, step, d_row


def _gather_blocks(ps_ref, u_scr):
    rows = u_scr.shape[0]
    for s in range(S5_BLOCK):
        for half in range(S5_CH // LANES):
            lo = s * S5_CH + half * LANES
            u_scr[:, lo:lo + LANES] = (
                ps_ref[half, 0, pl.ds(s, rows, stride=S5_BLOCK), :].astype(u_scr.dtype))


def _s5_scan_kernel(ps_ref, win_ref, step_ref, xs_ref, x_scr, bx_scr, u_scr):
    rows = xs_ref.shape[1]

    @pl.when(pl.program_id(1) == 0)
    def _():
        x_scr[...] = jnp.zeros_like(x_scr)

    _gather_blocks(ps_ref, u_scr)
    bx_scr[...] = _dot(u_scr[...], win_ref[...])
    a_same = step_ref[0:1, :]
    a_cross = step_ref[1:2, :]

    sub = 8

    def body(i, x):
        r0 = pl.multiple_of(i * sub, sub)
        bx = bx_scr[pl.ds(r0, sub), :]
        entering = []
        for r in range(sub):
            entering.append(x)
            swapped = jnp.concatenate([x[:, S5_NSTATE:], x[:, :S5_NSTATE]], axis=1)
            x = a_same * x + a_cross * swapped + bx[r:r + 1, :]
        xs_ref[0, pl.ds(r0, sub), :] = jnp.concatenate(entering, axis=0)
        return x

    x_scr[...] = lax.fori_loop(0, rows // sub, body, x_scr[...])


def _s5_out_kernel(ps_ref, xs_ref, m_ref, wout_ref, d_ref, y_ref, u_scr):
    rows = xs_ref.shape[1]
    _gather_blocks(ps_ref, u_scr)
    y = _dot(u_scr[...], m_ref[...]) + _dot(xs_ref[0].astype(BF16), wout_ref[...])
    for s in range(S5_BLOCK):
        tok = pl.ds(s, rows, stride=S5_BLOCK)
        for half in range(S5_CH // LANES):
            lo = s * S5_CH + half * LANES
            y_ref[half, 0, tok, :] = (y[:, lo:lo + LANES]
                                      + d_ref[:, half * LANES:(half + 1) * LANES] * ps_ref[half, 0, tok, :])


def _largest_divisor(n, candidates):
    for cand in candidates:
        if n % cand == 0:
            return cand
    return n


def _s5(ps, tables):
    m_w, win, wout, step, d_row = tables
    halves, b, lp, _ = ps.shape
    nb = lp // S5_BLOCK
    rows = _largest_divisor(nb, (704, 512, 256, 128, 64, 32, 16, 8))
    tok_spec = lambda r: pl.BlockSpec((halves, 1, r * S5_BLOCK, LANES), lambda i, j: (0, i, j, 0))
    blk_spec = lambda r: pl.BlockSpec((1, r, 2 * S5_NSTATE), lambda i, j: (i, j, 0))
    params = pltpu.CompilerParams(
        dimension_semantics=("arbitrary", "arbitrary"), vmem_limit_bytes=VMEM_LIMIT)
    xs = pl.pallas_call(
        _s5_scan_kernel,
        grid=(b, nb // rows),
        in_specs=[tok_spec(rows), _const_spec(win.shape), _const_spec(step.shape)],
        out_specs=blk_spec(rows),
        out_shape=jax.ShapeDtypeStruct((b, nb, 2 * S5_NSTATE), F32),
        scratch_shapes=[pltpu.VMEM((1, 2 * S5_NSTATE), F32),
                        pltpu.VMEM((rows, 2 * S5_NSTATE), F32),
                        pltpu.VMEM((rows, S5_ROW), BF16)],
        compiler_params=params,
        name="s5_scan",
    )(ps, win, step)
    rows = _largest_divisor(nb, (352, 256, 128, 64, 32, 16, 8))
    return pl.pallas_call(
        _s5_out_kernel,
        grid=(b, nb // rows),
        in_specs=[tok_spec(rows), blk_spec(rows), _const_spec(m_w.shape),
                  _const_spec(wout.shape), _const_spec(d_row.shape)],
        out_specs=tok_spec(rows),
        out_shape=jax.ShapeDtypeStruct((halves, b, lp, LANES), F32),
        scratch_shapes=[pltpu.VMEM((rows, S5_ROW), BF16)],
        compiler_params=params,
        name="s5_out",
    )(ps, xs, m_w, wout, d_row)


def _post_kernel(h_ref, head_ref, ya_ref, yb_ref, yc_ref, glu_ref, s5nw_ref, wo_ref, nw_ref,
                 w1_ref, w2_ref, fnw_ref, o_ref, *, final, per_seq):
    yc = jnp.concatenate([yc_ref[half] for half in range(S5_CH // LANES)], axis=1)
    zg = 0.5 * yc * (1.0 + jnp.tanh(0.7978845608028654 * (yc + 0.044715 * (yc * yc * yc))))
    out = zg * jax.nn.sigmoid(_dot(zg.astype(BF16), glu_ref[...]))
    y_c = _rmsnorm(out, s5nw_ref[...]).astype(BF16)
    h1 = _load_residual(h_ref, head_ref, per_seq)
    h1 = h1 + _dot(ya_ref[...], wo_ref[0:HG_W, :])
    h1 = h1 + _dot(yb_ref[...], wo_ref[HG_W:HG_W + GDN_W, :])
    h1 = h1 + _dot(y_c, wo_ref[HG_W + GDN_W:, :])
    hn = _rmsnorm(h1, nw_ref[...]).astype(BF16)
    mlp = None
    for j in range(D_FF // D_MODEL):
        sl = slice(j * D_MODEL, (j + 1) * D_MODEL)
        a = _dot(hn, w1_ref[:, sl])
        part = _dot(jnp.square(jnp.maximum(a, 0.0)).astype(BF16), w2_ref[sl, :])
        mlp = part if mlp is None else mlp + part
    acc = h1 + mlp
    if final:
        acc = _rmsnorm(acc, fnw_ref[...])
    o_ref[...] = acc


def _post(res, ya, yb, yc, glu, s5nw, wo, nw, w1, w2, fnw, final, drop_head_of=None):
    tm = TIME_BLOCK
    nblk = res.total_blocks(tm)
    m = nblk * tm
    row = lambda w: pl.BlockSpec((tm, w), lambda i: (i, 0))
    out_spec, out_rows = row(D_MODEL), m
    if drop_head_of is not None:
        out_rows = m // drop_head_of * (drop_head_of - 1)
        out_spec = _rows_spec(tm, drop_head_of)
    return pl.pallas_call(
        functools.partial(_post_kernel, final=final, per_seq=res.per_seq),
        grid=(nblk,),
        in_specs=res.specs(tm) + [row(HG_W), row(GDN_W),
                  pl.BlockSpec((S5_CH // LANES, tm, LANES), lambda i: (0, i, 0)),
                  _const_spec(glu.shape), _const_spec(s5nw.shape), _const_spec(wo.shape),
                  _const_spec(nw.shape), _const_spec(w1.shape), _const_spec(w2.shape),
                  _const_spec(fnw.shape)],
        out_specs=out_spec,
        out_shape=jax.ShapeDtypeStruct((out_rows, D_MODEL), F32),
        compiler_params=pltpu.CompilerParams(
            dimension_semantics=("arbitrary",), vmem_limit_bytes=VMEM_LIMIT),
        name="post",
    )(res.rows, res.head, ya, yb, yc, glu, s5nw, wo, nw, w1, w2, fnw)


def kernel(x, meta, norm_mix_w, norm_mlp_w, w_in, hgrn_lb_logits, hgrn_norm_w, gdn_conv_w, gdn_a_log, gdn_dt_bias, gdn_norm_w, s5_lam_re, s5_lam_im, s5_log_step, s5_b_re, s5_b_im, s5_c_re, s5_c_im, s5_d, s5_glu_w, s5_norm_w, w_out, mlp_w1, mlp_w2, final_norm_w):
    bsz, seq, d_model = x.shape
    assert d_model == D_MODEL
    depth = w_in.shape[0]
    l = N_META + seq
    lp = -(-l // TIME_BLOCK) * TIME_BLOCK
    front = lp - l
    head = jnp.concatenate([jnp.zeros((front, d_model), x.dtype), meta.astype(x.dtype)], axis=0)
    no_head = jnp.zeros((8, d_model), x.dtype)
    per_seq = lp // TIME_BLOCK
    head_is_block = lp - seq == TIME_BLOCK
    if head_is_block:
        res = _Residual(x.reshape(bsz * seq, d_model), head, per_seq)
    else:
        h = jnp.concatenate([jnp.broadcast_to(head[None], (bsz,) + head.shape), x], axis=1)
        res = _Residual(h.reshape(bsz * lp, d_model), no_head, None)

    lb_table = jnp.cumsum(jax.nn.softmax(hgrn_lb_logits.astype(F32), axis=0), axis=0)
    lb_table = lb_table - lb_table[0]

    o_hg, o_gdn = 4 * HG_W, 4 * HG_W + 4 * GDN_W
    o_s5 = o_gdn + 2 * GDN_HEADS
    for layer in range(depth):
        w = w_in[layer]
        wh = w[:, :o_hg].astype(BF16)
        wg = w[:, o_hg:o_gdn].astype(BF16)
        wab = jnp.pad(w[:, o_gdn:o_s5], ((0, 0), (0, LANES - 2 * GDN_HEADS))).astype(BF16)
        ws = w[:, o_s5:].astype(BF16)
        ph, pg, ps, pab = _inproj(res, per_seq, norm_mix_w[layer].reshape(1, -1),
                                  gdn_conv_w[layer].astype(F32), wh, wg, ws, wab)

        ya = _hgrn(ph.reshape(bsz, lp, -1), lb_table[layer].reshape(1, -1),
                   jnp.tile(hgrn_norm_w[layer].astype(F32), HG_HEADS).reshape(1, -1))

        ab = jnp.zeros((2, LANES), F32)
        ab = ab.at[0, :GDN_HEADS].set(-jnp.exp(gdn_a_log[layer].astype(F32)))
        ab = ab.at[1, :GDN_HEADS].set(gdn_dt_bias[layer].astype(F32))
        yb = _gdn(pg.reshape(bsz, lp, -1), pab.reshape(bsz, lp, -1), ab,
                  jnp.tile(gdn_norm_w[layer].astype(F32), GDN_HEADS).reshape(1, -1))

        tables = _s5_tables(s5_lam_re[layer], s5_lam_im[layer], s5_log_step[layer],
                            s5_b_re[layer], s5_b_im[layer], s5_c_re[layer], s5_c_im[layer],
                            s5_d[layer])
        yc = _s5(ps.reshape(-1, bsz, lp, LANES), tables).reshape(-1, bsz * lp, LANES)

        last = layer == depth - 1
        h2d = _post(res, ya.reshape(bsz * lp, -1), yb.reshape(bsz * lp, -1), yc,
                    s5_glu_w[layer].astype(BF16), s5_norm_w[layer].reshape(1, -1).astype(F32),
                    w_out[layer].astype(BF16), norm_mlp_w[layer].reshape(1, -1).astype(F32),
                    mlp_w1[layer].astype(BF16), mlp_w2[layer].astype(BF16),
                    final_norm_w.reshape(1, -1).astype(F32), final=last,
                    drop_head_of=per_seq if last and head_is_block else None)
        res = _Residual(h2d, no_head, None)
    if head_is_block:
        return h2d.reshape(bsz, seq, d_model)
    return h2d.reshape(bsz, lp, d_model)[:, lp - seq:]
```
